```python
import jax, jax.numpy as jnp
from jax import lax
import numpy as np

D_MODEL = 2048
BATCH = 16
SEQ = 2048
DEPTH = 4

HEAD_DIM = 64
D_MIX = D_MODEL
D_RWKV = D_MIX // 2
D_SB = D_MIX - D_RWKV
H_RWKV = D_RWKV // HEAD_DIM
H_SB = D_SB // HEAD_DIM
D_FF = 4 * D_MODEL
W_LORA = 64
A_LORA = 64
V_LORA = 32
G_LORA = 160
P_R0 = 3 * D_RWKV + W_LORA + A_LORA + G_LORA
P_SB = 3 * D_SB
P_IN = P_R0 + P_SB
BLOCK_Q = 128
NORM_EPS = 1e-6
LNX_EPS = 64e-5

kernel_name = "hybrid_rwkv7_stickbreaking_sandwich"


def rmsnorm(x, g):
    xf = x.astype(jnp.float32)
    xf = xf * lax.rsqrt(jnp.mean(xf * xf, axis=-1, keepdims=True) + NORM_EPS)
    return (xf * g.astype(jnp.float32)).astype(x.dtype)


def token_shift(p, mu):
    prev = jnp.pad(p[:, :-1], ((0, 0), (1, 0), (0, 0)))
    return p + (prev - p) * mu


def rwkv7_scan(r, w, k, v, kk, a):
    B, T, H, N = r.shape
    seq = [jnp.moveaxis(t, 1, 0).astype(jnp.float32) for t in (r, w, k, v, kk, a)]

    def step(S, inp):
        r_t, w_t, k_t, v_t, kk_t, a_t = inp
        sa = jnp.einsum('bhvk,bhk->bhv', S, -kk_t)
        S = (S * w_t[:, :, None, :]
             + sa[..., None] * (kk_t * a_t)[:, :, None, :]
             + v_t[..., None] * k_t[:, :, None, :])
        y = jnp.einsum('bhvk,bhk->bhv', S, r_t)
        return S, y

    S0 = jnp.zeros((B, H, N, N), jnp.float32)
    _, ys = lax.scan(step, S0, tuple(seq))
    return jnp.moveaxis(ys, 0, 1)


def rwkv7_time_mix(p, p_vres, v_first, mu, mu_vres, w0, w_up, a0, a_up, g_up,
                   v0, v_up, k_k, k_a, r_k, lnx_w, lnx_b):
    B, T, _ = p.shape
    xs = token_shift(p, mu)
    c = np.cumsum([D_RWKV, D_RWKV, D_RWKV, W_LORA, A_LORA])
    r, k, v, dw, da, dg = jnp.split(xs, [int(i) for i in c], axis=-1)
    w = -jax.nn.softplus(-(w0 + jnp.tanh(dw) @ w_up)) - 0.5
    decay = jnp.exp(-jnp.exp(w))
    a = jax.nn.sigmoid(a0 + da @ a_up)
    g = jax.nn.sigmoid(dg) @ g_up
    if v_first is None:
        v_first = v
    else:
        xv = token_shift(p_vres, mu_vres)
        v = v + (v_first - v) * jax.nn.sigmoid(v0 + xv @ v_up)
    heads = lambda t: t.reshape(B, T, H_RWKV, HEAD_DIM)
    kk = heads(k * k_k)
    kk = kk * lax.rsqrt(jnp.maximum(jnp.sum(kk * kk, axis=-1, keepdims=True), 1e-24))
    k = k * (1 + (a - 1) * k_a)
    r_h, k_h, v_h = heads(r), heads(k), heads(v)
    y = rwkv7_scan(r_h, heads(decay), k_h, v_h, kk, heads(a))
    mean = jnp.mean(y, axis=-1, keepdims=True)
    var = jnp.mean(jnp.square(y - mean), axis=-1, keepdims=True)
    yn = ((y - mean) * lax.rsqrt(var + LNX_EPS)).reshape(B, T, D_RWKV)
    yn = (yn * lnx_w + lnx_b).astype(p.dtype)
    bonus = (jnp.sum(r_h * k_h * r_k, axis=-1, keepdims=True) * v_h).reshape(B, T, D_RWKV)
    return (yn + bonus) * g, v_first


def stick_breaking_attention(q, k, v):
    B, T, H, N = q.shape
    scale = 1.0 / np.sqrt(N)
    outs = []
    for i in range(T // BLOCK_Q):
        q_lo, q_hi = i * BLOCK_Q, (i + 1) * BLOCK_Q
        z = jnp.einsum('bqhd,bkhd->bhqk', q[:, q_lo:q_hi], k[:, :q_hi],
                       preferred_element_type=jnp.float32) * scale
        t_idx = q_lo + jnp.arange(BLOCK_Q)[:, None]
        s_idx = jnp.arange(q_hi)[None, :]
        causal = s_idx < t_idx
        log_not = jnp.where(causal, jax.nn.log_sigmoid(-z), 0.0)
        between = lax.cumsum(log_not, axis=3, reverse=True) - log_not
        att = jnp.where(causal, jnp.exp(jax.nn.log_sigmoid(z) + between), 0.0)
        outs.append(jnp.einsum('bhqk,bkhd->bqhd', att.astype(v.dtype), v[:, :q_hi]))
    return jnp.concatenate(outs, axis=1)


def setup_inputs(seed: int = 0) -> dict:
    key = jax.random.key(seed)
    ks = iter(jax.random.split(key, 32))
    nrm = lambda shape, s: jax.random.normal(next(ks), shape, jnp.float32) * s
    gain = lambda shape: 1.0 + nrm(shape, 0.05)
    L1 = DEPTH - 1
    return {
        "x": nrm((BATCH, SEQ, D_MODEL), 1.0),
        "pre_mix_g": gain((DEPTH, D_MODEL)),
        "post_mix_g": gain((DEPTH, D_MODEL)),
        "pre_mlp_g": gain((DEPTH, D_MODEL)),
        "post_mlp_g": gain((DEPTH, D_MODEL)),
        "w_in": nrm((DEPTH, D_MODEL, P_IN), D_MODEL ** -0.5),
        "w_in_vres": nrm((L1, D_MODEL, V_LORA), D_MODEL ** -0.5),
        "mu": jax.random.uniform(next(ks), (DEPTH, P_R0), jnp.float32),
        "mu_vres": jax.random.uniform(next(ks), (L1, V_LORA), jnp.float32),
        "w0": jax.random.uniform(next(ks), (DEPTH, D_RWKV), jnp.float32, -6.0, 1.0),
        "w_up": nrm((DEPTH, W_LORA, D_RWKV), 0.1 * W_LORA ** -0.5),
        "a0": nrm((DEPTH, D_RWKV), 0.5),
        "a_up": nrm((DEPTH, A_LORA, D_RWKV), 0.5 * A_LORA ** -0.5),
        "g_up": nrm((DEPTH, G_LORA, D_RWKV), G_LORA ** -0.5),
        "v0": nrm((L1, D_RWKV), 0.5),
        "v_up": nrm((L1, V_LORA, D_RWKV), 0.5 * V_LORA ** -0.5),
        "k_k": 0.85 + nrm((DEPTH, D_RWKV), 0.05),
        "k_a": 1.0 + nrm((DEPTH, D_RWKV), 0.05),
        "r_k": nrm((DEPTH, H_RWKV, HEAD_DIM), 0.1),
        "lnx_w": gain((DEPTH, D_RWKV)),
        "lnx_b": nrm((DEPTH, D_RWKV), 0.02),
        "sb_out_g": gain((DEPTH, D_SB)),
        "w_out": nrm((DEPTH, D_MIX, D_MODEL), D_MIX ** -0.5),
        "w_ff_up": nrm((DEPTH, D_MODEL, D_FF), D_MODEL ** -0.5),
        "w_ff_down": nrm((DEPTH, D_FF, D_MODEL), D_FF ** -0.5),
    }


def reference(x, pre_mix_g, post_mix_g, pre_mlp_g, post_mlp_g, w_in, w_in_vres, mu, mu_vres,
              w0, w_up, a0, a_up, g_up, v0, v_up, k_k, k_a, r_k, lnx_w, lnx_b,
              sb_out_g, w_out, w_ff_up, w_ff_down):
    B, T, _ = x.shape
    v_first = None
    for l in range(DEPTH):
        h = rmsnorm(x, pre_mix_g[l])
        if l == 0:
            w_cat = w_in[0]
        else:
            w_cat = jnp.concatenate([w_in[l], w_in_vres[l - 1]], axis=1)
        proj = jnp.einsum('btd,dp->btp', h, w_cat)
        p_r = proj[..., :P_R0]
        p_sb = proj[..., P_R0:P_IN]
        if l == 0:
            y_r, v_first = rwkv7_time_mix(p_r, None, None, mu[l], None, w0[l], w_up[l], a0[l],
                                          a_up[l], g_up[l], None, None, k_k[l], k_a[l],
                                          r_k[l], lnx_w[l], lnx_b[l])
        else:
            y_r, v_first = rwkv7_time_mix(p_r, proj[..., P_IN:], v_first, mu[l], mu_vres[l - 1],
                                          w0[l], w_up[l], a0[l], a_up[l], g_up[l],
                                          v0[l - 1], v_up[l - 1], k_k[l], k_a[l], r_k[l],
                                          lnx_w[l], lnx_b[l])
        q, k, v = jnp.split(p_sb, 3, axis=-1)
        heads = lambda t: t.reshape(B, T, H_SB, HEAD_DIM)
        y_s = stick_breaking_attention(heads(q), heads(k), heads(v))
        y_s = rmsnorm(y_s, sb_out_g[l].reshape(H_SB, HEAD_DIM)).reshape(B, T, D_SB)
        mix = jnp.einsum('btc,cd->btd', jnp.concatenate([y_r, y_s], axis=-1), w_out[l])
        x = x + rmsnorm(mix, post_mix_g[l])
        h = rmsnorm(x, pre_mlp_g[l])
        ff = jnp.square(jax.nn.relu(jnp.einsum('btd,df->btf', h, w_ff_up[l])))
        ff = jnp.einsum('btf,fd->btd', ff, w_ff_down[l])
        x = x + rmsnorm(ff, post_mlp_g[l])
    return x
```

```python
import functools

import jax
import jax.numpy as jnp
from jax import lax
from jax.experimental import pallas as pl
from jax.experimental.pallas import tpu as pltpu

F32 = jnp.float32
BF16 = jnp.bfloat16

HEAD_DIM = 64
LANES = 128
NORM_EPS = 1e-6
LNX_EPS = 64e-5
CHUNK = 64
VMEM_LIMIT = 48 * 1024 * 1024

D_RWKV = 1024
D_SB = 1024
W_LORA, A_LORA, V_LORA, G_LORA = 64, 64, 32, 160
SEG_A = 0
SEG_C = 3 * D_RWKV
SEG_B = SEG_C + 256
SEG_D = SEG_B + 128
P_PAD = 6656


def _cparams(sem):
    return pltpu.CompilerParams(dimension_semantics=sem, vmem_limit_bytes=VMEM_LIMIT)


def _bdot(a, b):
    return jnp.dot(a.astype(BF16), b.astype(BF16), preferred_element_type=F32)


def _split2_dot(a_exact, x):
    hi = x.astype(BF16)
    lo = (x - hi.astype(F32)).astype(BF16)
    return (jnp.dot(a_exact, hi, preferred_element_type=F32)
            + jnp.dot(a_exact, lo, preferred_element_type=F32))


def _dot_split2(x, b_exact):
    hi = x.astype(BF16)
    lo = (x - hi.astype(F32)).astype(BF16)
    return (jnp.dot(hi, b_exact, preferred_element_type=F32)
            + jnp.dot(lo, b_exact, preferred_element_type=F32))


def _head_group_matrix(scale):
    i = lax.broadcasted_iota(jnp.int32, (LANES, LANES), 0) // HEAD_DIM
    j = lax.broadcasted_iota(jnp.int32, (LANES, LANES), 1) // HEAD_DIM
    return jnp.where(i == j, scale, 0.0).astype(BF16)


def _sigmoid(x):
    return 1.0 / (1.0 + jnp.exp(-x))


def _softplus(x):
    return jnp.maximum(x, 0.0) + jnp.log1p(jnp.exp(-jnp.abs(x)))


def _norm_matmul_kernel(x_ref, g_ref, w_ref, o_ref, h_ref, *, relu2):
    @pl.when(pl.program_id(1) == 0)
    def _():
        x = x_ref[...]
        ms = jnp.mean(x * x, axis=-1, keepdims=True)
        h_ref[...] = (x * lax.rsqrt(ms + NORM_EPS) * g_ref[...]).astype(BF16)

    y = jnp.dot(h_ref[...], w_ref[...], preferred_element_type=F32)
    if relu2:
        y = jnp.square(jnp.maximum(y, 0.0))
    o_ref[...] = y.astype(o_ref.dtype)


def _norm_matmul(x, g, w, *, relu2, out_dtype, tm, tn):
    n, d = x.shape
    p = w.shape[1]
    return pl.pallas_call(
        functools.partial(_norm_matmul_kernel, relu2=relu2),
        grid=(n // tm, p // tn),
        in_specs=[
            pl.BlockSpec((tm, d), lambda i, j: (i, 0)),
            pl.BlockSpec((1, d), lambda i, j: (0, 0)),
            pl.BlockSpec((d, tn), lambda i, j: (0, j)),
        ],
        out_specs=pl.BlockSpec((tm, tn), lambda i, j: (i, j)),
        out_shape=jax.ShapeDtypeStruct((n, p), out_dtype),
        scratch_shapes=[pltpu.VMEM((tm, d), BF16)],
        compiler_params=_cparams(("parallel", "arbitrary")),
        name="norm_matmul_relu2" if relu2 else "norm_matmul",
    )(x, g.reshape(1, d), w)


def _norm_residual(acc, x, g):
    ms = jnp.mean(acc * acc, axis=-1, keepdims=True)
    return x + acc * lax.rsqrt(ms + NORM_EPS) * g


def _mix_out_kernel(a1_ref, a2_ref, w1_ref, w2_ref, x_ref, g_ref, o_ref):
    acc = jnp.dot(a1_ref[...], w1_ref[...], preferred_element_type=F32)
    acc = acc + jnp.dot(a2_ref[...], w2_ref[...], preferred_element_type=F32)
    o_ref[...] = _norm_residual(acc, x_ref[...], g_ref[...])


def _mix_out(a1, a2, w, x, g, *, tm):
    n, d = x.shape
    k1 = a1.shape[1]
    k2 = a2.shape[1]
    assert k1 == k2 and w.shape[0] == k1 + k2
    return pl.pallas_call(
        _mix_out_kernel,
        grid=(n // tm,),
        in_specs=[
            pl.BlockSpec((tm, k1), lambda i: (i, 0)),
            pl.BlockSpec((tm, k2), lambda i: (i, 0)),
            pl.BlockSpec((k1, d), lambda i: (0, 0)),
            pl.BlockSpec((k2, d), lambda i: (1, 0)),
            pl.BlockSpec((tm, d), lambda i: (i, 0)),
            pl.BlockSpec((1, d), lambda i: (0, 0)),
        ],
        out_specs=pl.BlockSpec((tm, d), lambda i: (i, 0)),
        out_shape=jax.ShapeDtypeStruct((n, d), F32),
        compiler_params=_cparams(("parallel",)),
        name="mix_out",
    )(a1, a2, w, w, x, g.reshape(1, d))


def _ffn_down_kernel(a_ref, w_ref, x_ref, g_ref, o_ref, acc_ref, *, nk):
    k = pl.program_id(1)

    @pl.when(k == 0)
    def _():
        acc_ref[...] = jnp.zeros_like(acc_ref)

    acc_ref[...] += jnp.dot(a_ref[...], w_ref[...], preferred_element_type=F32)

    @pl.when(k == nk - 1)
    def _():
        o_ref[...] = _norm_residual(acc_ref[...], x_ref[...], g_ref[...])


def _ffn_down(a, w, x, g, *, tm, tk):
    n, d = x.shape
    kdim = a.shape[1]
    nk = kdim // tk
    return pl.pallas_call(
        functools.partial(_ffn_down_kernel, nk=nk),
        grid=(n // tm, nk),
        in_specs=[
            pl.BlockSpec((tm, tk), lambda i, k: (i, k)),
            pl.BlockSpec((tk, d), lambda i, k: (k, 0)),
            pl.BlockSpec((tm, d), lambda i, k: (i, 0)),
            pl.BlockSpec((1, d), lambda i, k: (0, 0)),
        ],
        out_specs=pl.BlockSpec((tm, d), lambda i, k: (i, 0)),
        out_shape=jax.ShapeDtypeStruct((n, d), F32),
        scratch_shapes=[pltpu.VMEM((tm, d), F32)],
        compiler_params=_cparams(("parallel", "arbitrary")),
        name="ffn_down",
    )(a, w, x, g.reshape(1, d))


def _prep_kernel(*refs, has_vres, tt):
    if has_vres:
        (pa_ref, pc_ref, pb_ref, mua_ref, muc_ref, mub_ref, w0_ref, wup_ref, a0_ref, aup_ref,
         gup_ref, kk_ref, ka_ref, vf_ref, v0_ref, vup_ref,
         r_out, k_out, v_out, lw_out, kk_out, a_out, g_out,
         ca_ref, cc_ref, cb_ref) = refs
    else:
        (pa_ref, pc_ref, pb_ref, mua_ref, muc_ref, mub_ref, w0_ref, wup_ref, a0_ref, aup_ref,
         gup_ref, kk_ref, ka_ref,
         r_out, k_out, v_out, lw_out, kk_out, a_out, g_out,
         ca_ref, cc_ref, cb_ref) = refs

    @pl.when(pl.program_id(1) == 0)
    def _():
        ca_ref[...] = jnp.zeros_like(ca_ref)
        cc_ref[...] = jnp.zeros_like(cc_ref)
        cb_ref[...] = jnp.zeros_like(cb_ref)

    def shifted(p, prev_row, mu):
        rolled = pltpu.roll(p, 1, axis=0)
        row = lax.broadcasted_iota(jnp.int32, p.shape, 0)
        prev = jnp.where(row == 0, prev_row, rolled)
        return p + (prev - p) * mu

    pb = pb_ref[...].astype(F32)
    xb = shifted(pb, cb_ref[7:8, :], mub_ref[...])
    pc = pc_ref[...].astype(F32)
    xc = shifted(pc, cc_ref[7:8, :], muc_ref[...])

    wl = w0_ref[...] + _bdot(jnp.tanh(xb), wup_ref[...])
    w = -_softplus(-wl) - 0.5
    lw_out[...] = -jnp.exp(w)
    a = _sigmoid(a0_ref[...] + _bdot(xb, aup_ref[...]))
    a_out[...] = a
    g_out[...] = _bdot(_sigmoid(xc), gup_ref[...])

    d = D_RWKV
    pr = pa_ref[:, 0:d].astype(F32)
    r_out[...] = shifted(pr, ca_ref[7:8, 0:d], mua_ref[:, 0:d])
    pk = pa_ref[:, d:2 * d].astype(F32)
    k = shifted(pk, ca_ref[7:8, d:2 * d], mua_ref[:, d:2 * d])
    kk_out[...] = k * kk_ref[...]
    k_out[...] = k * (1.0 + (a - 1.0) * ka_ref[...])
    pv = pa_ref[:, 2 * d:3 * d].astype(F32)
    v = shifted(pv, ca_ref[7:8, 2 * d:3 * d], mua_ref[:, 2 * d:3 * d])
    if has_vres:
        mix = _sigmoid(v0_ref[...] + _bdot(xc, vup_ref[...]))
        v = v + (vf_ref[...] - v) * mix
    v_out[...] = v

    ca_ref[...] = pa_ref[tt - 8:tt, :].astype(F32)
    cc_ref[...] = pc[tt - 8:tt, :]
    cb_ref[...] = pb[tt - 8:tt, :]


def _rwkv_prep(proj, vfirst, lp, *, bsz, seq, tt):
    n = bsz * seq
    nt = seq // tt
    has_vres = vfirst is not None
    d = D_RWKV
    row = lambda b, t: (b * nt + t, 0)
    const = lambda b, t: (0, 0)
    in_specs = [
        pl.BlockSpec((tt, 3 * d), lambda b, t: (b * nt + t, SEG_A // (3 * d))),
        pl.BlockSpec((tt, 256), lambda b, t: (b * nt + t, SEG_C // 256)),
        pl.BlockSpec((tt, 128), lambda b, t: (b * nt + t, SEG_B // 128)),
        pl.BlockSpec((1, 3 * d), const),
        pl.BlockSpec((1, 256), const),
        pl.BlockSpec((1, 128), const),
        pl.BlockSpec((1, d), const),
        pl.BlockSpec((128, d), const),
        pl.BlockSpec((1, d), const),
        pl.BlockSpec((128, d), const),
        pl.BlockSpec((256, d), const),
        pl.BlockSpec((1, d), const),
        pl.BlockSpec((1, d), const),
    ]
    args = [proj, proj, proj, lp["mu_a"], lp["mu_c"], lp["mu_b"], lp["w0"], lp["wup"], lp["a0"],
            lp["aup"], lp["gup"], lp["k_k"], lp["k_a"]]
    if has_vres:
        in_specs += [pl.BlockSpec((tt, d), row), pl.BlockSpec((1, d), const),
                     pl.BlockSpec((256, d), const)]
        args += [vfirst, lp["v0"], lp["vup"]]
    out_sd = jax.ShapeDtypeStruct((n, d), F32)
    return pl.pallas_call(
        functools.partial(_prep_kernel, has_vres=has_vres, tt=tt),
        grid=(bsz, nt),
        in_specs=in_specs,
        out_specs=[pl.BlockSpec((tt, d), row)] * 7,
        out_shape=[out_sd] * 7,
        scratch_shapes=[pltpu.VMEM((8, 3 * d), F32), pltpu.VMEM((8, 256), F32),
                        pltpu.VMEM((8, 128), F32)],
        compiler_params=_cparams(("parallel", "arbitrary")),
        name="rwkv_prep",
    )(*args)


def _unit_lower_inverse(lp, d16):
    n = lp.shape[0]
    eye = (lax.broadcasted_iota(jnp.int32, (n, n), 0)
           == lax.broadcasted_iota(jnp.int32, (n, n), 1)).astype(F32)
    ld = jnp.where(d16, lp, 0.0)
    e = lp - ld
    l2 = _bdot(ld, ld)
    p = eye + ld
    p = p + _bdot(p, l2)
    l4 = _bdot(l2, l2)
    p = p + _bdot(p, l4)
    l8 = _bdot(l4, l4)
    dinv = p + _bdot(p, l8)
    nm = _bdot(dinv, e)
    n2 = _bdot(nm, nm)
    a1 = dinv + _bdot(nm, dinv)
    return a1 + _bdot(n2, a1)


def _scan_kernel(r_ref, k_ref, v_ref, lw_ref, kk_ref, a_ref, g_ref, rk_ref, lnw_ref, lnb_ref,
                 y_ref, s_ref, ybuf_ref, *, tt):
    c = CHUNK

    @pl.when(pl.program_id(2) == 0)
    def _():
        s_ref[...] = jnp.zeros_like(s_ref)

    row = lax.broadcasted_iota(jnp.int32, (c, LANES), 0)
    lane = lax.broadcasted_iota(jnp.int32, (c, LANES), 1)
    is_a = lane < HEAD_DIM
    s_idx = lane % HEAD_DIM
    strict = s_idx < row
    incl = s_idx <= row
    r2 = lax.broadcasted_iota(jnp.int32, (2 * c, LANES), 0)
    l2 = lax.broadcasted_iota(jnp.int32, (2 * c, LANES), 1)
    d16 = (r2 // 16) == (l2 // 16)
    same_head = (r2 // HEAD_DIM) == (l2 // HEAD_DIM)
    tri_incl = (lax.broadcasted_iota(jnp.int32, (c, c), 1)
                <= lax.broadcasted_iota(jnp.int32, (c, c), 0)).astype(BF16)
    gsum = _head_group_matrix(1.0)
    nt_dims = (((1,), (1,)), ((), ()))

    for ci in range(tt // c):
        sl = slice(ci * c, (ci + 1) * c)
        r = r_ref[sl, :]
        k = k_ref[sl, :]
        v = v_ref[sl, :]
        lw = lw_ref[sl, :]
        kkr = kk_ref[sl, :]
        kk = kkr * lax.rsqrt(jnp.maximum(_dot_split2(kkr * kkr, gsum), 1e-24))
        b = kk * a_ref[sl, :]

        c_in = _split2_dot(tri_incl, lw)
        e_tot = jnp.exp(c_in[c - 1:c, :])
        e_neg = jnp.exp(-c_in)
        at = -kk * jnp.exp(c_in - lw)
        rt = r * jnp.exp(c_in)
        bt = b * e_neg
        kt = k * e_neg

        lhs_a = jnp.concatenate([jnp.where(is_a, at, 0.0), jnp.where(is_a, rt, 0.0)], axis=0)
        lhs_b = jnp.concatenate([jnp.where(is_a, 0.0, at), jnp.where(is_a, 0.0, rt)], axis=0)
        btb = bt.astype(BF16)
        ktb = kt.astype(BF16)
        g_a = lax.dot_general(lhs_a.astype(BF16), jnp.concatenate([btb, ktb], axis=0), nt_dims,
                              preferred_element_type=F32)
        g_b = lax.dot_general(lhs_b.astype(BF16), jnp.concatenate([ktb, btb], axis=0), nt_dims,
                              preferred_element_type=F32)
        ga_top, ga_bot = g_a[0:c], g_a[c:2 * c]
        gb_top, gb_bot = g_b[0:c], g_b[c:2 * c]
        lp = jnp.concatenate([jnp.where(is_a & strict, ga_top, 0.0),
                              jnp.where((~is_a) & strict, gb_top, 0.0)], axis=0)
        lak = jnp.where(strict, jnp.where(is_a, gb_top, ga_top), 0.0)
        mrk = jnp.where(incl, jnp.where(is_a, gb_bot, ga_bot), 0.0)
        mrb = jnp.where(incl, jnp.where(is_a, ga_bot, gb_bot), 0.0)
        tp = _unit_lower_inverse(lp, d16)

        s0 = s_ref[...]
        ah = lax.dot_general(jnp.concatenate([at, rt], axis=0).astype(BF16), s0.astype(BF16),
                             nt_dims, preferred_element_type=F32)
        v_sw = jnp.concatenate([jnp.where(is_a, 0.0, v), jnp.where(is_a, v, 0.0)], axis=0)
        lv = _bdot(jnp.concatenate([lak, mrk], axis=0), v_sw)
        x = ah[0:c] + lv[0:c]
        x_st = jnp.concatenate([jnp.where(is_a, x, 0.0), jnp.where(is_a, 0.0, x)], axis=0)
        tu = _bdot(tp, x_st)
        u = tu[0:c] + tu[c:2 * c]
        u_st = jnp.concatenate([jnp.where(is_a, u, 0.0), jnp.where(is_a, 0.0, u)], axis=0)
        ybuf_ref[sl, :] = ah[c:2 * c] + lv[c:2 * c] + _bdot(mrb, u_st)

        uv_t = jnp.concatenate([u, v], axis=0).T
        bk_h = jnp.concatenate([bt * e_tot, kt * e_tot], axis=0)
        s_ref[...] = s0 * e_tot + jnp.where(same_head, _bdot(uv_t, bk_h), 0.0)

    y = ybuf_ref[...]
    gavg = _head_group_matrix(1.0 / HEAD_DIM)
    mean = _dot_split2(y, gavg)
    yc = y - mean
    var = _dot_split2(yc * yc, gavg)
    yn = yc * lax.rsqrt(var + LNX_EPS) * lnw_ref[...] + lnb_ref[...]
    bonus = _dot_split2(r_ref[...] * k_ref[...] * rk_ref[...], gsum) * v_ref[...]
    y_ref[...] = ((yn + bonus) * g_ref[...]).astype(y_ref.dtype)


def _rwkv_scan(r, k, v, lw, kk, a, g, r_k, lnx_w, lnx_b, *, bsz, seq, tt):
    n, d = r.shape
    nt = seq // tt
    npair = d // LANES
    blk = pl.BlockSpec((tt, LANES), lambda b, h, t: (b * nt + t, h))
    par = pl.BlockSpec((1, LANES), lambda b, h, t: (0, h))
    return pl.pallas_call(
        functools.partial(_scan_kernel, tt=tt),
        grid=(bsz, npair, nt),
        in_specs=[blk] * 7 + [par] * 3,
        out_specs=blk,
        out_shape=jax.ShapeDtypeStruct((n, d), BF16),
        scratch_shapes=[pltpu.VMEM((LANES, LANES), F32), pltpu.VMEM((tt, LANES), F32)],
        compiler_params=_cparams(("parallel", "parallel", "arbitrary")),
        name="rwkv_scan",
    )(r, k, v, lw, kk, a, g, r_k.reshape(1, d), lnx_w.reshape(1, d), lnx_b.reshape(1, d))


def _sb_kernel(q_ref, k_ref, v_ref, g_ref, o_ref, *, blk, scale):
    i = pl.program_id(2)
    lane = lax.broadcasted_iota(jnp.int32, (blk, LANES), 1)
    is_a = lane < HEAD_DIM
    tq = lax.broadcasted_iota(jnp.int32, (blk, blk), 0)
    sk = lax.broadcasted_iota(jnp.int32, (blk, blk), 1)
    causal = sk < tq
    suffix = (tq > sk).astype(BF16)
    suffix2 = jnp.concatenate([suffix, suffix], axis=0)
    nt_dims = (((1,), (1,)), ((), ()))

    q = q_ref[...].astype(F32) * scale
    q_heads = (jnp.where(is_a, q, 0.0).astype(BF16), jnp.where(is_a, 0.0, q).astype(BF16))

    def block_step(j, carries, acc, diag):
        start = pl.multiple_of(j * blk, blk)
        kb = k_ref[pl.ds(start, blk), :].astype(BF16)
        vb = v_ref[pl.ds(start, blk), :].astype(F32)
        atts = []
        new_carries = []
        for qh, carry in zip(q_heads, carries):
            z = lax.dot_general(qh, kb, nt_dims, preferred_element_type=F32)
            ln = -_softplus(z)
            if diag:
                ln = jnp.where(causal, ln, 0.0)
            hi = ln.astype(BF16)
            lo = (ln - hi.astype(F32)).astype(BF16)
            between = jnp.dot(jnp.concatenate([hi, lo], axis=1), suffix2,
                              preferred_element_type=F32) + carry
            att = jnp.exp(z + ln + between)
            if diag:
                att = jnp.where(causal, att, 0.0)
            atts.append(att.astype(BF16))
            new_carries.append(carry + jnp.sum(ln, axis=1, keepdims=True))
        v_st = jnp.concatenate([jnp.where(is_a, vb, 0.0), jnp.where(is_a, 0.0, vb)], axis=0)
        acc = acc + jnp.dot(jnp.concatenate(atts, axis=1), v_st.astype(BF16),
                            preferred_element_type=F32)
        return tuple(new_carries), acc

    zero_c = jnp.zeros((blk, 1), F32)
    carries, acc = block_step(i, (zero_c, zero_c), jnp.zeros((blk, LANES), F32), True)

    def body(jj, state):
        ca, cb, acc = state
        (ca, cb), acc = block_step(i - 1 - jj, (ca, cb), acc, False)
        return ca, cb, acc

    _, _, acc = lax.fori_loop(0, i, body, (carries[0], carries[1], acc))

    ms = _dot_split2(acc * acc, _head_group_matrix(1.0 / HEAD_DIM))
    o_ref[...] = (acc * lax.rsqrt(ms + NORM_EPS) * g_ref[...]).astype(o_ref.dtype)


def _sb_attention(proj, gain, *, bsz, seq, blk):
    n = bsz * seq
    nq = seq // blk
    npair = D_SB // LANES
    q0 = SEG_D // LANES
    return pl.pallas_call(
        functools.partial(_sb_kernel, blk=blk, scale=1.0 / (HEAD_DIM ** 0.5)),
        grid=(bsz, npair, nq),
        in_specs=[
            pl.BlockSpec((blk, LANES), lambda b, h, i: (b * nq + i, q0 + h)),
            pl.BlockSpec((seq, LANES), lambda b, h, i: (b, q0 + npair + h)),
            pl.BlockSpec((seq, LANES), lambda b, h, i: (b, q0 + 2 * npair + h)),
            pl.BlockSpec((1, LANES), lambda b, h, i: (0, h)),
        ],
        out_specs=pl.BlockSpec((blk, LANES), lambda b, h, i: (b * nq + i, h)),
        out_shape=jax.ShapeDtypeStruct((n, D_SB), BF16),
        compiler_params=_cparams(("parallel", "parallel", "arbitrary")),
        name="sb_attention",
    )(proj, proj, proj, gain.reshape(1, D_SB))


def _layer_params(l, d_model, w_in, w_in_vres, mu, mu_vres, w0, w_up, a0, a_up, g_up, v0, v_up,
                  k_k, k_a):
    d = D_RWKV
    wl = w_in[l]
    c_dw = 3 * d
    c_da = c_dw + W_LORA
    c_dg = c_da + A_LORA
    c_sb = c_dg + G_LORA
    zcol = lambda m: jnp.zeros((d_model, m), F32)
    zrow = lambda m: jnp.zeros((m,), F32)
    if l > 0:
        vres_w, vres_mu = w_in_vres[l - 1], mu_vres[l - 1]
    else:
        vres_w, vres_mu = zcol(V_LORA), zrow(V_LORA)
    w_cat = jnp.concatenate([
        wl[:, 0:c_dw],
        wl[:, c_dg:c_sb], vres_w, zcol(256 - G_LORA - V_LORA),
        wl[:, c_dw:c_dg],
        wl[:, c_sb:],
        zcol(P_PAD - SEG_D - 3 * D_SB),
    ], axis=1).astype(BF16)
    mul = mu[l]
    lp = {
        "w_cat": w_cat,
        "mu_a": mul[0:c_dw].reshape(1, -1),
        "mu_c": jnp.concatenate([mul[c_dg:c_sb], vres_mu,
                                 zrow(256 - G_LORA - V_LORA)]).reshape(1, -1),
        "mu_b": mul[c_dw:c_dg].reshape(1, -1),
        "w0": w0[l].reshape(1, d),
        "a0": a0[l].reshape(1, d),
        "k_k": k_k[l].reshape(1, d),
        "k_a": k_a[l].reshape(1, d),
        "wup": jnp.concatenate([w_up[l], jnp.zeros((A_LORA, d), F32)], axis=0).astype(BF16),
        "aup": jnp.concatenate([jnp.zeros((W_LORA, d), F32), a_up[l]], axis=0).astype(BF16),
        "gup": jnp.concatenate([g_up[l], jnp.zeros((256 - G_LORA, d), F32)],
                               axis=0).astype(BF16),
    }
    if l > 0:
        lp["v0"] = v0[l - 1].reshape(1, d)
        lp["vup"] = jnp.concatenate([jnp.zeros((G_LORA, d), F32), v_up[l - 1],
                                     jnp.zeros((256 - G_LORA - V_LORA, d), F32)],
                                    axis=0).astype(BF16)
    return lp


def kernel(x, pre_mix_g, post_mix_g, pre_mlp_g, post_mlp_g, w_in, w_in_vres, mu, mu_vres, w0, w_up,
           a0, a_up, g_up, v0, v_up, k_k, k_a, r_k, lnx_w, lnx_b, sb_out_g, w_out, w_ff_up,
           w_ff_down):
    bsz, seq, d_model = x.shape
    depth = w_in.shape[0]
    n = bsz * seq
    assert w_in.shape[2] == 3 * D_RWKV + W_LORA + A_LORA + G_LORA + 3 * D_SB
    tm = min(512, n)
    tt = min(256, seq)
    xf = x.reshape(n, d_model)
    vfirst = None
    for l in range(depth):
        lp = _layer_params(l, d_model, w_in, w_in_vres, mu, mu_vres, w0, w_up, a0, a_up, g_up,
                           v0, v_up, k_k, k_a)
        proj = _norm_matmul(xf, pre_mix_g[l], lp["w_cat"], relu2=False, out_dtype=F32,
                            tm=tm, tn=512)
        r, k, v, lw, kk, a, g = _rwkv_prep(proj, vfirst, lp, bsz=bsz, seq=seq, tt=tt)
        if l == 0:
            vfirst = v
        y_r = _rwkv_scan(r, k, v, lw, kk, a, g, r_k[l], lnx_w[l], lnx_b[l],
                         bsz=bsz, seq=seq, tt=tt)
        y_s = _sb_attention(proj, sb_out_g[l], bsz=bsz, seq=seq, blk=min(128, seq))
        xf = _mix_out(y_r, y_s, w_out[l].astype(BF16), xf, post_mix_g[l], tm=tm)
        ff = _norm_matmul(xf, pre_mlp_g[l], w_ff_up[l].astype(BF16), relu2=True, out_dtype=BF16,
                          tm=tm, tn=512)
        xf = _ffn_down(ff, w_ff_down[l].astype(BF16), xf, post_mlp_g[l], tm=tm, tk=1024)
    return xf.reshape(bsz, seq, d_model)
```

```python
import functools

import jax
import jax.numpy as jnp
from jax import lax
from jax.experimental import pallas as pl
from jax.experimental.pallas import tpu as pltpu

F32 = jnp.float32
BF16 = jnp.bfloat16

HEAD_DIM = 64
LANES = 128
NORM_EPS = 1e-6
LNX_EPS = 64e-5
CHUNK = 64
VMEM_LIMIT = 48 * 1024 * 1024
LOG2E = 1.4426950408889634

D_RWKV = 1024
D_SB = 1024
W_LORA, A_LORA, V_LORA, G_LORA = 64, 64, 32, 160
SEG_A = 0
SEG_C = 3 * D_RWKV
SEG_B = SEG_C + 256
P_RWKV = 3584

NT_DIMS = (((1,), (1,)), ((), ()))


def _cparams(sem):
    return pltpu.CompilerParams(dimension_semantics=sem, vmem_limit_bytes=VMEM_LIMIT)


def _bdot(a, b):
    return jnp.dot(a.astype(BF16), b.astype(BF16), preferred_element_type=F32)


def _split2_dot(a_exact, x):
    hi = x.astype(BF16)
    lo = (x - hi.astype(F32)).astype(BF16)
    return (jnp.dot(a_exact, hi, preferred_element_type=F32)
            + jnp.dot(a_exact, lo, preferred_element_type=F32))


def _head_group_matrix(scale):
    i = lax.broadcasted_iota(jnp.int32, (LANES, LANES), 0) // HEAD_DIM
    j = lax.broadcasted_iota(jnp.int32, (LANES, LANES), 1) // HEAD_DIM
    return jnp.where(i == j, scale, 0.0).astype(BF16)


def _sigmoid(x):
    return 1.0 / (1.0 + jnp.exp(-x))


def _softplus(x):
    return jnp.maximum(x, 0.0) + jnp.log(1.0 + jnp.exp(-jnp.abs(x)))


def _neg_abs(x):
    bits = lax.bitcast_convert_type(x, jnp.uint32) | jnp.uint32(0x80000000)
    return lax.bitcast_convert_type(bits, F32)


def _softplus2(z2):
    return jnp.maximum(z2, 0.0) + jnp.log2(1.0 + jnp.exp2(_neg_abs(z2)))


def _each(fn, *lists):
    return [fn(*xs) for xs in zip(*lists)]


def _rms_scaled(x, g):
    ms = jnp.mean(x * x, axis=-1, keepdims=True)
    return x * lax.rsqrt(ms + NORM_EPS) * g


def _norm_cast_kernel(x_ref, g_ref, h_ref):
    h_ref[...] = _rms_scaled(x_ref[...], g_ref[...]).astype(h_ref.dtype)


def _norm_cast(x, g, *, tm):
    n, d = x.shape
    return pl.pallas_call(
        _norm_cast_kernel,
        grid=(n // tm,),
        in_specs=[pl.BlockSpec((tm, d), lambda i: (i, 0)), pl.BlockSpec((1, d), lambda i: (0, 0))],
        out_specs=pl.BlockSpec((tm, d), lambda i: (i, 0)),
        out_shape=jax.ShapeDtypeStruct((n, d), BF16),
        compiler_params=_cparams(("parallel",)),
        name="norm_cast",
    )(x, g.reshape(1, d))


def _matmul_kernel(a_ref, w_ref, o_ref, *, relu2):
    y = jnp.dot(a_ref[...], w_ref[...], preferred_element_type=F32)
    if relu2:
        y = jnp.square(jnp.maximum(y, 0.0))
    o_ref[...] = y.astype(o_ref.dtype)


def _matmul(a, w, *, relu2, tm, tn, name):
    n, kdim = a.shape
    p = w.shape[1]
    return pl.pallas_call(
        functools.partial(_matmul_kernel, relu2=relu2),
        grid=(n // tm, p // tn),
        in_specs=[
            pl.BlockSpec((tm, kdim), lambda i, j: (i, 0)),
            pl.BlockSpec((kdim, tn), lambda i, j: (0, j)),
        ],
        out_specs=pl.BlockSpec((tm, tn), lambda i, j: (i, j)),
        out_shape=jax.ShapeDtypeStruct((n, p), BF16),
        compiler_params=_cparams(("parallel", "arbitrary")),
        name=name,
    )(a, w)


def _residual_epilogue(acc, x_ref, g_ref, gn_ref, o_ref, h_ref):
    xn = x_ref[...] + _rms_scaled(acc, g_ref[...])
    o_ref[...] = xn
    if h_ref is not None:
        h_ref[...] = _rms_scaled(xn, gn_ref[...]).astype(h_ref.dtype)


def _mix_out_kernel(a1_ref, a2_ref, w1_ref, w2_ref, x_ref, g_ref, gn_ref, o_ref, h_ref):
    acc = jnp.dot(a1_ref[...], w1_ref[...], preferred_element_type=F32)
    acc = acc + jnp.dot(a2_ref[...], w2_ref[...], preferred_element_type=F32)
    _residual_epilogue(acc, x_ref, g_ref, gn_ref, o_ref, h_ref)


def _mix_out(a1, a2, w, x, g, g_next, *, tm):
    n, d = x.shape
    k1 = a1.shape[1]
    k2 = a2.shape[1]
    assert k1 == k2 and w.shape[0] == k1 + k2
    return pl.pallas_call(
        _mix_out_kernel,
        grid=(n // tm,),
        in_specs=[
            pl.BlockSpec((tm, k1), lambda i: (i, 0)),
            pl.BlockSpec((tm, k2), lambda i: (i, 0)),
            pl.BlockSpec((k1, d), lambda i: (0, 0)),
            pl.BlockSpec((k2, d), lambda i: (1, 0)),
            pl.BlockSpec((tm, d), lambda i: (i, 0)),
            pl.BlockSpec((1, d), lambda i: (0, 0)),
            pl.BlockSpec((1, d), lambda i: (0, 0)),
        ],
        out_specs=[pl.BlockSpec((tm, d), lambda i: (i, 0))] * 2,
        out_shape=[jax.ShapeDtypeStruct((n, d), F32), jax.ShapeDtypeStruct((n, d), BF16)],
        compiler_params=_cparams(("parallel",)),
        name="mix_out",
    )(a1, a2, w, w, x, g.reshape(1, d), g_next.reshape(1, d))


def _ffn_down_kernel(*refs, nk, emit_h):
    if emit_h:
        a_ref, w_ref, x_ref, g_ref, gn_ref, o_ref, h_ref = refs
    else:
        a_ref, w_ref, x_ref, g_ref, o_ref = refs
        gn_ref = h_ref = None
    k = pl.program_id(1)
    part = jnp.dot(a_ref[...], w_ref[...], preferred_element_type=F32)

    @pl.when(k == 0)
    def _():
        o_ref[...] = part

    @pl.when(k > 0)
    def _():
        o_ref[...] += part

    @pl.when(k == nk - 1)
    def _():
        _residual_epilogue(o_ref[...], x_ref, g_ref, gn_ref, o_ref, h_ref)


def _ffn_down(a, w, x, g, g_next, *, tm, tk):
    n, d = x.shape
    kdim = a.shape[1]
    nk = kdim // tk
    emit_h = g_next is not None
    row = pl.BlockSpec((tm, d), lambda i, k: (i, 0))
    gain = pl.BlockSpec((1, d), lambda i, k: (0, 0))
    in_specs = [pl.BlockSpec((tm, tk), lambda i, k: (i, k)),
                pl.BlockSpec((tk, d), lambda i, k: (k, 0)), row, gain]
    args = [a, w, x, g.reshape(1, d)]
    out_specs = [row]
    out_shape = [jax.ShapeDtypeStruct((n, d), F32)]
    if emit_h:
        in_specs.append(gain)
        args.append(g_next.reshape(1, d))
        out_specs.append(row)
        out_shape.append(jax.ShapeDtypeStruct((n, d), BF16))
    outs = pl.pallas_call(
        functools.partial(_ffn_down_kernel, nk=nk, emit_h=emit_h),
        grid=(n // tm, nk),
        in_specs=in_specs,
        out_specs=out_specs,
        out_shape=out_shape,
        compiler_params=_cparams(("parallel", "arbitrary")),
        name="ffn_down",
    )(*args)
    return (outs[0], outs[1]) if emit_h else (outs[0], None)


def _prep_kernel(*refs, has_vres, tt):
    if has_vres:
        (pa_ref, pc_ref, pb_ref, mua_ref, muc_ref, mub_ref, w0_ref, wup_ref, a0_ref, aup_ref,
         gup_ref, kk_ref, ka_ref, vf_ref, v0_ref, vup_ref,
         r_out, k_out, v_out, lw_out, kk_out, a_out, g_out,
         ca_ref, cc_ref, cb_ref) = refs
    else:
        (pa_ref, pc_ref, pb_ref, mua_ref, muc_ref, mub_ref, w0_ref, wup_ref, a0_ref, aup_ref,
         gup_ref, kk_ref, ka_ref,
         r_out, k_out, v_out, lw_out, kk_out, a_out, g_out,
         ca_ref, cc_ref, cb_ref) = refs

    @pl.when(pl.program_id(1) == 0)
    def _():
        ca_ref[...] = jnp.zeros_like(ca_ref)
        cc_ref[...] = jnp.zeros_like(cc_ref)
        cb_ref[...] = jnp.zeros_like(cb_ref)

    def shifted(p, prev_row, mu):
        rolled = pltpu.roll(p, 1, axis=0)
        row = lax.broadcasted_iota(jnp.int32, p.shape, 0)
        prev = jnp.where(row == 0, prev_row, rolled)
        return p + (prev - p) * mu

    pb = pb_ref[...].astype(F32)
    xb = shifted(pb, cb_ref[7:8, :], mub_ref[...])
    pc = pc_ref[...].astype(F32)
    xc = shifted(pc, cc_ref[7:8, :], muc_ref[...])

    wl = w0_ref[...] + _bdot(jnp.tanh(xb), wup_ref[...])
    w = -_softplus(-wl) - 0.5
    lw_out[...] = -jnp.exp(w)
    a = _sigmoid(a0_ref[...] + _bdot(xb, aup_ref[...]))
    a_out[...] = a.astype(a_out.dtype)
    g_out[...] = _bdot(_sigmoid(xc), gup_ref[...]).astype(g_out.dtype)

    d = D_RWKV
    pr = pa_ref[:, 0:d].astype(F32)
    r_out[...] = shifted(pr, ca_ref[7:8, 0:d], mua_ref[:, 0:d]).astype(r_out.dtype)
    pk = pa_ref[:, d:2 * d].astype(F32)
    k = shifted(pk, ca_ref[7:8, d:2 * d], mua_ref[:, d:2 * d])
    kk_out[...] = (k * kk_ref[...]).astype(kk_out.dtype)
    k_out[...] = (k * (1.0 + (a - 1.0) * ka_ref[...])).astype(k_out.dtype)
    pv = pa_ref[:, 2 * d:3 * d].astype(F32)
    v = shifted(pv, ca_ref[7:8, 2 * d:3 * d], mua_ref[:, 2 * d:3 * d])
    if has_vres:
        mix = _sigmoid(v0_ref[...] + _bdot(xc, vup_ref[...]))
        v = v + (vf_ref[...].astype(F32) - v) * mix
    v_out[...] = v.astype(v_out.dtype)

    ca_ref[...] = pa_ref[tt - 8:tt, :].astype(F32)
    cc_ref[...] = pc[tt - 8:tt, :]
    cb_ref[...] = pb[tt - 8:tt, :]


def _rwkv_prep(proj, vfirst, lp, *, bsz, seq, tt):
    n = bsz * seq
    nt = seq // tt
    has_vres = vfirst is not None
    d = D_RWKV
    row = lambda b, t: (b * nt + t, 0)
    const = lambda b, t: (0, 0)
    in_specs = [
        pl.BlockSpec((tt, 3 * d), lambda b, t: (b * nt + t, SEG_A // (3 * d))),
        pl.BlockSpec((tt, 256), lambda b, t: (b * nt + t, SEG_C // 256)),
        pl.BlockSpec((tt, 128), lambda b, t: (b * nt + t, SEG_B // 128)),
        pl.BlockSpec((1, 3 * d), const),
        pl.BlockSpec((1, 256), const),
        pl.BlockSpec((1, 128), const),
        pl.BlockSpec((1, d), const),
        pl.BlockSpec((128, d), const),
        pl.BlockSpec((1, d), const),
        pl.BlockSpec((128, d), const),
        pl.BlockSpec((256, d), const),
        pl.BlockSpec((1, d), const),
        pl.BlockSpec((1, d), const),
    ]
    args = [proj, proj, proj, lp["mu_a"], lp["mu_c"], lp["mu_b"], lp["w0"], lp["wup"], lp["a0"],
            lp["aup"], lp["gup"], lp["k_k"], lp["k_a"]]
    if has_vres:
        in_specs += [pl.BlockSpec((tt, d), row), pl.BlockSpec((1, d), const),
                     pl.BlockSpec((256, d), const)]
        args += [vfirst, lp["v0"], lp["vup"]]
    sd = lambda dt: jax.ShapeDtypeStruct((n, d), dt)
    return pl.pallas_call(
        functools.partial(_prep_kernel, has_vres=has_vres, tt=tt),
        grid=(bsz, nt),
        in_specs=in_specs,
        out_specs=[pl.BlockSpec((tt, d), row)] * 7,
        out_shape=[sd(BF16), sd(BF16), sd(BF16), sd(F32), sd(BF16), sd(BF16), sd(BF16)],
        scratch_shapes=[pltpu.VMEM((8, 3 * d), F32), pltpu.VMEM((8, 256), F32),
                        pltpu.VMEM((8, 128), F32)],
        compiler_params=_cparams(("parallel", "arbitrary")),
        name="rwkv_prep",
    )(*args)


def _unit_lower_inverses(lps, d16):
    n = lps[0].shape[0]
    eye = (lax.broadcasted_iota(jnp.int32, (n, n), 0)
           == lax.broadcasted_iota(jnp.int32, (n, n), 1)).astype(F32)
    ld = [jnp.where(d16, lp, 0.0) for lp in lps]
    e = _each(lambda a, b: a - b, lps, ld)
    l2 = _each(_bdot, ld, ld)
    p = [eye + x for x in ld]
    p = _each(lambda a, b: a + _bdot(a, b), p, l2)
    l4 = _each(_bdot, l2, l2)
    p = _each(lambda a, b: a + _bdot(a, b), p, l4)
    l8 = _each(_bdot, l4, l4)
    dinv = _each(lambda a, b: a + _bdot(a, b), p, l8)
    nm = _each(_bdot, dinv, e)
    n2 = _each(_bdot, nm, nm)
    a1 = _each(lambda a, b: a + _bdot(b, a), dinv, nm)
    return _each(lambda a, b: a + _bdot(b, a), a1, n2)


def _scan_kernel(r_ref, k_ref, v_ref, lw_ref, kk_ref, a_ref, g_ref, rk_ref, lnw_ref, lnb_ref,
                 y_ref, s_ref, *, tt, npair):
    c = CHUNK
    nch = tt // c

    @pl.when(pl.program_id(1) == 0)
    def _():
        s_ref[...] = jnp.zeros_like(s_ref)

    row = lax.broadcasted_iota(jnp.int32, (c, LANES), 0)
    lane = lax.broadcasted_iota(jnp.int32, (c, LANES), 1)
    is_a = lane < HEAD_DIM
    s_idx = lane % HEAD_DIM
    strict = s_idx < row
    incl = s_idx <= row
    strict_a = is_a & strict
    strict_b = (~is_a) & strict
    r2 = lax.broadcasted_iota(jnp.int32, (2 * c, LANES), 0)
    l2 = lax.broadcasted_iota(jnp.int32, (2 * c, LANES), 1)
    d16 = (r2 // 16) == (l2 // 16)
    same_head = (r2 // HEAD_DIM) == (l2 // HEAD_DIM)
    tri_incl = (lax.broadcasted_iota(jnp.int32, (c, c), 1)
                <= lax.broadcasted_iota(jnp.int32, (c, c), 0)).astype(BF16)
    gsum = _head_group_matrix(1.0)

    def head_stack(x):
        return jnp.concatenate([jnp.where(is_a, x, 0.0), jnp.where(is_a, 0.0, x)], axis=0)

    def head_stack_swapped(x):
        return jnp.concatenate([jnp.where(is_a, 0.0, x), jnp.where(is_a, x, 0.0)], axis=0)

    elems = [(ci, p) for ci in range(nch) for p in range(npair)]

    def tiles(ref):
        return [ref[ci * c:(ci + 1) * c, p * LANES:(p + 1) * LANES].astype(F32)
                for ci, p in elems]

    kkr = tiles(kk_ref)
    lw = tiles(lw_ref)
    ss = [_bdot(x * x, gsum) for x in kkr]
    c_in = [_split2_dot(tri_incl, x) for x in lw]
    kk = _each(lambda x, s: x * lax.rsqrt(jnp.maximum(s, 1e-24)), kkr, ss)
    b = _each(lambda x, a: x * a, kk, tiles(a_ref))
    e_tot = [jnp.exp(x[c - 1:c, :]) for x in c_in]
    e_neg = [jnp.exp(-x) for x in c_in]
    at = _each(lambda x, ci_, l: -x * jnp.exp(ci_ - l), kk, c_in, lw)
    rt = _each(lambda x, ci_: x * jnp.exp(ci_), tiles(r_ref), c_in)
    bt = _each(lambda x, en: x * en, b, e_neg)
    kt = _each(lambda x, en: x * en, tiles(k_ref), e_neg)
    v = tiles(v_ref)
    ar = _each(lambda x, y: jnp.concatenate([x, y], axis=0).astype(BF16), at, rt)
    bkt = _each(lambda x, y: jnp.concatenate([x, y], axis=0).astype(BF16), bt, kt)
    lhs = [jnp.concatenate([jnp.where(l2 < HEAD_DIM, x, jnp.zeros_like(x)),
                            jnp.where(l2 < HEAD_DIM, jnp.zeros_like(x), x)], axis=0) for x in ar]
    g_ab = _each(lambda x, y: lax.dot_general(x, y, NT_DIMS, preferred_element_type=F32),
                 lhs, bkt)
    g_a = [x[0:2 * c] for x in g_ab]
    g_b = [pltpu.roll(x[2 * c:4 * c], HEAD_DIM, axis=1) for x in g_ab]
    lps = _each(lambda ga, gb: jnp.concatenate([jnp.where(strict_a, ga[0:c], 0.0),
                                                jnp.where(strict_b, gb[0:c], 0.0)], axis=0),
                g_a, g_b)
    lmk = _each(lambda ga, gb: jnp.concatenate(
        [jnp.where(strict, jnp.where(is_a, gb[0:c], ga[0:c]), 0.0),
         jnp.where(incl, jnp.where(is_a, gb[c:2 * c], ga[c:2 * c]), 0.0)], axis=0), g_a, g_b)
    mrb = _each(lambda ga, gb: jnp.where(incl, jnp.where(is_a, ga[c:2 * c], gb[c:2 * c]),
                                         0.0).astype(BF16), g_a, g_b)
    lv = _each(lambda m, x: _bdot(m, head_stack_swapped(x)), lmk, v)
    bk_h = _each(lambda x, y, et: jnp.concatenate([x * et, y * et], axis=0).astype(BF16),
                 bt, kt, e_tot)
    tps = [x.astype(BF16) for x in _unit_lower_inverses(lps, d16)]

    s = [s_ref[p] for p in range(npair)]
    ys = []
    for ci in range(nch):
        idx = [ci * npair + p for p in range(npair)]
        ah = [lax.dot_general(ar[i], s[p].astype(BF16), NT_DIMS, preferred_element_type=F32)
              for p, i in enumerate(idx)]
        x = [ah[p][0:c] + lv[i][0:c] for p, i in enumerate(idx)]
        tu = [jnp.dot(tps[i], head_stack(x[p]).astype(BF16), preferred_element_type=F32)
              for p, i in enumerate(idx)]
        u = [t[0:c] + t[c:2 * c] for t in tu]
        ys.append([ah[p][c:2 * c] + lv[i][c:2 * c]
                   + jnp.dot(mrb[i], head_stack(u[p]).astype(BF16), preferred_element_type=F32)
                   for p, i in enumerate(idx)])
        uv_t = [jnp.concatenate([u[p], v[i]], axis=0).T.astype(BF16)
                for p, i in enumerate(idx)]
        s = [s[p] * e_tot[i]
             + jnp.where(same_head, jnp.dot(uv_t[p], bk_h[i], preferred_element_type=F32), 0.0)
             for p, i in enumerate(idx)]
    for p in range(npair):
        s_ref[p] = s[p]

    gavg = _head_group_matrix(1.0 / HEAD_DIM)
    for p in range(npair):
        cols = slice(p * LANES, (p + 1) * LANES)
        y = jnp.concatenate([ys[ci][p] for ci in range(nch)], axis=0)
        yc = y - _bdot(y, gavg)
        var = _bdot(yc * yc, gavg)
        yn = yc * lax.rsqrt(var + LNX_EPS) * lnw_ref[:, cols] + lnb_ref[:, cols]
        rk = r_ref[:, cols].astype(F32) * k_ref[:, cols].astype(F32) * rk_ref[:, cols]
        bonus = _bdot(rk, gsum) * v_ref[:, cols].astype(F32)
        y_ref[:, cols] = ((yn + bonus) * g_ref[:, cols].astype(F32)).astype(y_ref.dtype)


def _rwkv_scan(r, k, v, lw, kk, a, g, r_k, lnx_w, lnx_b, *, bsz, seq, tt):
    n, d = r.shape
    nt = seq // tt
    npair = d // LANES
    blk = pl.BlockSpec((tt, d), lambda b, t: (b * nt + t, 0))
    par = pl.BlockSpec((1, d), lambda b, t: (0, 0))
    return pl.pallas_call(
        functools.partial(_scan_kernel, tt=tt, npair=npair),
        grid=(bsz, nt),
        in_specs=[blk] * 7 + [par] * 3,
        out_specs=blk,
        out_shape=jax.ShapeDtypeStruct((n, d), BF16),
        scratch_shapes=[pltpu.VMEM((npair, LANES, LANES), F32)],
        compiler_params=_cparams(("parallel", "arbitrary")),
        name="rwkv_scan",
    )(r, k, v, lw, kk, a, g, r_k.reshape(1, d), lnx_w.reshape(1, d), lnx_b.reshape(1, d))


def _sb_kernel(q_ref, k_ref, v_ref, g_ref, o_ref, vab_ref, *, blk, nblk, scale):
    i = pl.program_id(2)
    lane = lax.broadcasted_iota(jnp.int32, (blk, LANES), 1)
    is_a = lane < HEAD_DIM

    @pl.when(i == 0)
    def _():
        for j in range(nblk):
            vb = v_ref[j * blk:(j + 1) * blk, :]
            vab_ref[j, 0:blk, :] = jnp.where(is_a, vb, jnp.zeros_like(vb))
            vab_ref[j, blk:2 * blk, :] = jnp.where(is_a, jnp.zeros_like(vb), vb)

    tq = lax.broadcasted_iota(jnp.int32, (blk, blk), 0)
    sk = lax.broadcasted_iota(jnp.int32, (blk, blk), 1)
    causal = sk < tq
    suffix = (tq > sk).astype(BF16)

    q = q_ref[...].astype(F32) * (scale * LOG2E)
    q_heads = (jnp.where(is_a, q, 0.0).astype(BF16), jnp.where(is_a, 0.0, q).astype(BF16))

    def block_step(j, carries, acc, diag):
        start = pl.multiple_of(j * blk, blk)
        kb = k_ref[pl.ds(start, blk), :]
        atts = []
        new_carries = []
        for qh, carry in zip(q_heads, carries):
            z2 = lax.dot_general(qh, kb, NT_DIMS, preferred_element_type=F32)
            sp = _softplus2(z2)
            if diag:
                sp = jnp.where(causal, sp, 0.0)
            between = jnp.dot(sp.astype(BF16), suffix, preferred_element_type=F32)
            att = jnp.exp2(z2 - sp - between - carry)
            if diag:
                att = jnp.where(causal, att, 0.0)
            atts.append(att.astype(BF16))
            new_carries.append(carry + jnp.sum(sp, axis=1, keepdims=True))
        acc = acc + jnp.dot(jnp.concatenate(atts, axis=1), vab_ref[j],
                            preferred_element_type=F32)
        return tuple(new_carries), acc

    def full_step(j, state):
        (ca, cb), acc = block_step(j, (state[0], state[1]), state[2], False)
        return ca, cb, acc

    zero_c = jnp.zeros((blk, 1), F32)
    (ca, cb), acc = block_step(i, (zero_c, zero_c), jnp.zeros((blk, LANES), F32), True)
    odd = i % 2
    state = lax.cond(odd == 1, lambda s: full_step(i - 1, s), lambda s: s, (ca, cb, acc))

    def body(m, s):
        j = i - 1 - odd - 2 * m
        return full_step(j - 1, full_step(j, s))

    _, _, acc = lax.fori_loop(0, i // 2, body, state)

    ms = _bdot(acc * acc, _head_group_matrix(1.0 / HEAD_DIM))
    o_ref[...] = (acc * lax.rsqrt(ms + NORM_EPS) * g_ref[...]).astype(o_ref.dtype)


def _sb_attention(qkv, gain, *, bsz, seq, blk):
    n = bsz * seq
    nq = seq // blk
    npair = D_SB // LANES
    return pl.pallas_call(
        functools.partial(_sb_kernel, blk=blk, nblk=nq, scale=1.0 / (HEAD_DIM ** 0.5)),
        grid=(bsz, npair, nq),
        in_specs=[
            pl.BlockSpec((blk, LANES), lambda b, h, i: (b * nq + i, h)),
            pl.BlockSpec((seq, LANES), lambda b, h, i: (b, npair + h)),
            pl.BlockSpec((seq, LANES), lambda b, h, i: (b, 2 * npair + h)),
            pl.BlockSpec((1, LANES), lambda b, h, i: (0, h)),
        ],
        out_specs=pl.BlockSpec((blk, LANES), lambda b, h, i: (b * nq + i, h)),
        out_shape=jax.ShapeDtypeStruct((n, D_SB), BF16),
        scratch_shapes=[pltpu.VMEM((nq, 2 * blk, LANES), BF16)],
        compiler_params=_cparams(("parallel", "parallel", "arbitrary")),
        name="sb_attention",
    )(qkv, qkv, qkv, gain.reshape(1, D_SB))


def _layer_params(l, d_model, w_in, w_in_vres, mu, mu_vres, w0, w_up, a0, a_up, g_up, v0, v_up,
                  k_k, k_a):
    d = D_RWKV
    wl = w_in[l]
    c_dw = 3 * d
    c_da = c_dw + W_LORA
    c_dg = c_da + A_LORA
    c_sb = c_dg + G_LORA
    zcol = lambda m: jnp.zeros((d_model, m), F32)
    zrow = lambda m: jnp.zeros((m,), F32)
    if l > 0:
        vres_w, vres_mu = w_in_vres[l - 1], mu_vres[l - 1]
    else:
        vres_w, vres_mu = zcol(V_LORA), zrow(V_LORA)
    w_rwkv = jnp.concatenate([
        wl[:, 0:c_dw],
        wl[:, c_dg:c_sb], vres_w, zcol(256 - G_LORA - V_LORA),
        wl[:, c_dw:c_dg],
        zcol(P_RWKV - SEG_B - 128),
    ], axis=1).astype(BF16)
    mul = mu[l]
    lp = {
        "w_rwkv": w_rwkv,
        "w_sb": wl[:, c_sb:].astype(BF16),
        "mu_a": mul[0:c_dw].reshape(1, -1),
        "mu_c": jnp.concatenate([mul[c_dg:c_sb], vres_mu,
                                 zrow(256 - G_LORA - V_LORA)]).reshape(1, -1),
        "mu_b": mul[c_dw:c_dg].reshape(1, -1),
        "w0": w0[l].reshape(1, d),
        "a0": a0[l].reshape(1, d),
        "k_k": k_k[l].reshape(1, d),
        "k_a": k_a[l].reshape(1, d),
        "wup": jnp.concatenate([w_up[l], jnp.zeros((A_LORA, d), F32)], axis=0).astype(BF16),
        "aup": jnp.concatenate([jnp.zeros((W_LORA, d), F32), a_up[l]], axis=0).astype(BF16),
        "gup": jnp.concatenate([g_up[l], jnp.zeros((256 - G_LORA, d), F32)],
                               axis=0).astype(BF16),
    }
    if l > 0:
        lp["v0"] = v0[l - 1].reshape(1, d)
        lp["vup"] = jnp.concatenate([jnp.zeros((G_LORA, d), F32), v_up[l - 1],
                                     jnp.zeros((256 - G_LORA - V_LORA, d), F32)],
                                    axis=0).astype(BF16)
    return lp


def kernel(x, pre_mix_g, post_mix_g, pre_mlp_g, post_mlp_g, w_in, w_in_vres, mu, mu_vres, w0, w_up,
           a0, a_up, g_up, v0, v_up, k_k, k_a, r_k, lnx_w, lnx_b, sb_out_g, w_out, w_ff_up,
           w_ff_down):
    bsz, seq, d_model = x.shape
    depth = w_in.shape[0]
    n = bsz * seq
    assert w_in.shape[2] == 3 * D_RWKV + W_LORA + A_LORA + G_LORA + 3 * D_SB
    tm_mm = min(1024, n)
    tm_res = min(512, n)
    xf = x.reshape(n, d_model)
    h = _norm_cast(xf, pre_mix_g[0], tm=tm_res)
    vfirst = None
    for l in range(depth):
        lp = _layer_params(l, d_model, w_in, w_in_vres, mu, mu_vres, w0, w_up, a0, a_up, g_up,
                           v0, v_up, k_k, k_a)
        proj = _matmul(h, lp["w_rwkv"], relu2=False, tm=tm_mm, tn=512, name="proj_rwkv")
        qkv = _matmul(h, lp["w_sb"], relu2=False, tm=tm_mm, tn=1024, name="proj_sb")
        r, k, v, lw, kk, a, g = _rwkv_prep(proj, vfirst, lp, bsz=bsz, seq=seq, tt=min(256, seq))
        if l == 0:
            vfirst = v
        y_r = _rwkv_scan(r, k, v, lw, kk, a, g, r_k[l], lnx_w[l], lnx_b[l],
                         bsz=bsz, seq=seq, tt=min(128, seq))
        y_s = _sb_attention(qkv, sb_out_g[l], bsz=bsz, seq=seq, blk=min(256, seq))
        xf, h = _mix_out(y_r, y_s, w_out[l].astype(BF16), xf, post_mix_g[l], pre_mlp_g[l],
                         tm=tm_res)
        ff = _matmul(h, w_ff_up[l].astype(BF16), relu2=True, tm=tm_mm, tn=1024, name="ffn_up")
        g_next = pre_mix_g[l + 1] if l + 1 < depth else None
        xf, h = _ffn_down(ff, w_ff_down[l].astype(BF16), xf, post_mlp_g[l], g_next,
                          tm=tm_res, tk=2048)
    return xf.reshape(bsz, seq, d_model)
```

```python
import functools

import jax
import jax.numpy as jnp
from jax import lax
from jax.experimental import pallas as pl
from jax.experimental.pallas import tpu as pltpu

F32 = jnp.float32
BF16 = jnp.bfloat16

HEAD_DIM = 64
LANES = 128
NORM_EPS = 1e-6
LNX_EPS = 64e-5
CHUNK = 64
VMEM_LIMIT = 48 * 1024 * 1024
LOG2E = 1.4426950408889634

D_RWKV = 1024
D_SB = 1024
W_LORA, A_LORA, V_LORA, G_LORA = 64, 64, 32, 160
SEG_A = 0
SEG_C = 3 * D_RWKV
SEG_B = SEG_C + 256
P_RWKV = 3584

NT_DIMS = (((1,), (1,)), ((), ()))


def _cparams(sem):
    return pltpu.CompilerParams(dimension_semantics=sem, vmem_limit_bytes=VMEM_LIMIT)


def _bdot(a, b):
    return jnp.dot(a.astype(BF16), b.astype(BF16), preferred_element_type=F32)


def _split2_dot(a_exact, x):
    hi = x.astype(BF16)
    lo = (x - hi.astype(F32)).astype(BF16)
    return (jnp.dot(a_exact, hi, preferred_element_type=F32)
            + jnp.dot(a_exact, lo, preferred_element_type=F32))


def _head_group_matrix(scale):
    i = lax.broadcasted_iota(jnp.int32, (LANES, LANES), 0) // HEAD_DIM
    j = lax.broadcasted_iota(jnp.int32, (LANES, LANES), 1) // HEAD_DIM
    return jnp.where(i == j, scale, 0.0).astype(BF16)


def _sigmoid(x):
    return 1.0 / (1.0 + jnp.exp(-x))


def _softplus(x):
    return jnp.maximum(x, 0.0) + jnp.log(1.0 + jnp.exp(-jnp.abs(x)))


def _neg_abs(x):
    bits = lax.bitcast_convert_type(x, jnp.uint32) | jnp.uint32(0x80000000)
    return lax.bitcast_convert_type(bits, F32)


def _softplus2(z2):
    return jnp.maximum(z2, 0.0) + jnp.log2(1.0 + jnp.exp2(_neg_abs(z2)))


def _each(fn, *lists):
    return [fn(*xs) for xs in zip(*lists)]


def _rms_scaled(x, g):
    ms = jnp.mean(x * x, axis=-1, keepdims=True)
    return x * lax.rsqrt(ms + NORM_EPS) * g


def _norm_cast_kernel(x_ref, g_ref, h_ref):
    h_ref[...] = _rms_scaled(x_ref[...], g_ref[...]).astype(h_ref.dtype)


def _norm_cast(x, g, *, tm):
    n, d = x.shape
    return pl.pallas_call(
        _norm_cast_kernel,
        grid=(n // tm,),
        in_specs=[pl.BlockSpec((tm, d), lambda i: (i, 0)), pl.BlockSpec((1, d), lambda i: (0, 0))],
        out_specs=pl.BlockSpec((tm, d), lambda i: (i, 0)),
        out_shape=jax.ShapeDtypeStruct((n, d), BF16),
        compiler_params=_cparams(("parallel",)),
        name="norm_cast",
    )(x, g.reshape(1, d))


def _matmul_kernel(a_ref, w_ref, o_ref, *, relu2):
    y = jnp.dot(a_ref[...], w_ref[...], preferred_element_type=F32)
    if relu2:
        y = jnp.square(jnp.maximum(y, 0.0))
    o_ref[...] = y.astype(o_ref.dtype)


def _matmul(a, w, *, relu2, tm, tn, name):
    n, kdim = a.shape
    p = w.shape[1]
    return pl.pallas_call(
        functools.partial(_matmul_kernel, relu2=relu2),
        grid=(n // tm, p // tn),
        in_specs=[
            pl.BlockSpec((tm, kdim), lambda i, j: (i, 0)),
            pl.BlockSpec((kdim, tn), lambda i, j: (0, j)),
        ],
        out_specs=pl.BlockSpec((tm, tn), lambda i, j: (i, j)),
        out_shape=jax.ShapeDtypeStruct((n, p), BF16),
        compiler_params=_cparams(("parallel", "arbitrary")),
        name=name,
    )(a, w)


def _residual_epilogue(acc, x_ref, g_ref, gn_ref, o_ref, h_ref):
    xn = x_ref[...] + _rms_scaled(acc, g_ref[...])
    o_ref[...] = xn
    if h_ref is not None:
        h_ref[...] = _rms_scaled(xn, gn_ref[...]).astype(h_ref.dtype)


def _mix_out_kernel(a1_ref, a2_ref, w1_ref, w2_ref, x_ref, g_ref, gn_ref, o_ref, h_ref):
    acc = jnp.dot(a1_ref[...], w1_ref[...], preferred_element_type=F32)
    acc = acc + jnp.dot(a2_ref[...], w2_ref[...], preferred_element_type=F32)
    _residual_epilogue(acc, x_ref, g_ref, gn_ref, o_ref, h_ref)


def _mix_out(a1, a2, w, x, g, g_next, *, tm):
    n, d = x.shape
    k1 = a1.shape[1]
    k2 = a2.shape[1]
    assert k1 == k2 and w.shape[0] == k1 + k2
    return pl.pallas_call(
        _mix_out_kernel,
        grid=(n // tm,),
        in_specs=[
            pl.BlockSpec((tm, k1), lambda i: (i, 0)),
            pl.BlockSpec((tm, k2), lambda i: (i, 0)),
            pl.BlockSpec((k1, d), lambda i: (0, 0)),
            pl.BlockSpec((k2, d), lambda i: (1, 0)),
            pl.BlockSpec((tm, d), lambda i: (i, 0)),
            pl.BlockSpec((1, d), lambda i: (0, 0)),
            pl.BlockSpec((1, d), lambda i: (0, 0)),
        ],
        out_specs=[pl.BlockSpec((tm, d), lambda i: (i, 0))] * 2,
        out_shape=[jax.ShapeDtypeStruct((n, d), F32), jax.ShapeDtypeStruct((n, d), BF16)],
        compiler_params=_cparams(("parallel",)),
        name="mix_out",
    )(a1, a2, w, w, x, g.reshape(1, d), g_next.reshape(1, d))


def _ffn_down_kernel(*refs, nk, emit_h):
    if emit_h:
        a_ref, w_ref, x_ref, g_ref, gn_ref, o_ref, h_ref = refs
    else:
        a_ref, w_ref, x_ref, g_ref, o_ref = refs
        gn_ref = h_ref = None
    k = pl.program_id(1)
    part = jnp.dot(a_ref[...], w_ref[...], preferred_element_type=F32)

    @pl.when(k == 0)
    def _():
        o_ref[...] = part

    @pl.when(k > 0)
    def _():
        o_ref[...] += part

    @pl.when(k == nk - 1)
    def _():
        _residual_epilogue(o_ref[...], x_ref, g_ref, gn_ref, o_ref, h_ref)


def _ffn_down(a, w, x, g, g_next, *, tm, tk):
    n, d = x.shape
    kdim = a.shape[1]
    nk = kdim // tk
    emit_h = g_next is not None
    row = pl.BlockSpec((tm, d), lambda i, k: (i, 0))
    gain = pl.BlockSpec((1, d), lambda i, k: (0, 0))
    in_specs = [pl.BlockSpec((tm, tk), lambda i, k: (i, k)),
                pl.BlockSpec((tk, d), lambda i, k: (k, 0)), row, gain]
    args = [a, w, x, g.reshape(1, d)]
    out_specs = [row]
    out_shape = [jax.ShapeDtypeStruct((n, d), F32)]
    if emit_h:
        in_specs.append(gain)
        args.append(g_next.reshape(1, d))
        out_specs.append(row)
        out_shape.append(jax.ShapeDtypeStruct((n, d), BF16))
    outs = pl.pallas_call(
        functools.partial(_ffn_down_kernel, nk=nk, emit_h=emit_h),
        grid=(n // tm, nk),
        in_specs=in_specs,
        out_specs=out_specs,
        out_shape=out_shape,
        compiler_params=_cparams(("parallel", "arbitrary")),
        name="ffn_down",
    )(*args)
    return (outs[0], outs[1]) if emit_h else (outs[0], None)


def _prep_compute(in_refs, out_refs, carry_refs, *, has_vres, tt):
    (pa_ref, pc_ref, pb_ref, mua_ref, muc_ref, mub_ref, w0_ref, wup_ref, a0_ref, aup_ref,
     gup_ref, kk_ref, ka_ref) = in_refs[0:13]
    if has_vres:
        vf_ref, v0_ref, vup_ref = in_refs[13:16]
    r_out, k_out, v_out, lw_out, kk_out, a_out, g_out = out_refs
    ca_ref, cc_ref, cb_ref = carry_refs

    @pl.when(pl.program_id(1) == 0)
    def _():
        ca_ref[...] = jnp.zeros_like(ca_ref)
        cc_ref[...] = jnp.zeros_like(cc_ref)
        cb_ref[...] = jnp.zeros_like(cb_ref)

    def shifted(p, prev_row, mu):
        rolled = pltpu.roll(p, 1, axis=0)
        row = lax.broadcasted_iota(jnp.int32, p.shape, 0)
        prev = jnp.where(row == 0, prev_row, rolled)
        return p + (prev - p) * mu

    pb = pb_ref[...].astype(F32)
    xb = shifted(pb, cb_ref[7:8, :], mub_ref[...])
    pc = pc_ref[...].astype(F32)
    xc = shifted(pc, cc_ref[7:8, :], muc_ref[...])

    wl = w0_ref[...] + _bdot(jnp.tanh(xb), wup_ref[...])
    w = -_softplus(-wl) - 0.5
    lw_out[...] = -jnp.exp(w)
    a = _sigmoid(a0_ref[...] + _bdot(xb, aup_ref[...]))
    a_out[...] = a.astype(a_out.dtype)
    g_out[...] = _bdot(_sigmoid(xc), gup_ref[...]).astype(g_out.dtype)

    d = D_RWKV
    pr = pa_ref[:, 0:d].astype(F32)
    r_out[...] = shifted(pr, ca_ref[7:8, 0:d], mua_ref[:, 0:d]).astype(r_out.dtype)
    pk = pa_ref[:, d:2 * d].astype(F32)
    k = shifted(pk, ca_ref[7:8, d:2 * d], mua_ref[:, d:2 * d])
    kk_out[...] = (k * kk_ref[...]).astype(kk_out.dtype)
    k_out[...] = (k * (1.0 + (a - 1.0) * ka_ref[...])).astype(k_out.dtype)
    pv = pa_ref[:, 2 * d:3 * d].astype(F32)
    v = shifted(pv, ca_ref[7:8, 2 * d:3 * d], mua_ref[:, 2 * d:3 * d])
    if has_vres:
        mix = _sigmoid(v0_ref[...] + _bdot(xc, vup_ref[...]))
        v = v + (vf_ref[...].astype(F32) - v) * mix
    v_out[...] = v.astype(v_out.dtype)

    ca_ref[...] = pa_ref[tt - 8:tt, :].astype(F32)
    cc_ref[...] = pc[tt - 8:tt, :]
    cb_ref[...] = pb[tt - 8:tt, :]


def _unit_lower_inverses(lps, d16):
    n = lps[0].shape[0]
    eye = (lax.broadcasted_iota(jnp.int32, (n, n), 0)
           == lax.broadcasted_iota(jnp.int32, (n, n), 1)).astype(F32)
    ld = [jnp.where(d16, lp, 0.0) for lp in lps]
    e = _each(lambda a, b: a - b, lps, ld)
    l2 = _each(_bdot, ld, ld)
    p = [eye + x for x in ld]
    p = _each(lambda a, b: a + _bdot(a, b), p, l2)
    l4 = _each(_bdot, l2, l2)
    p = _each(lambda a, b: a + _bdot(a, b), p, l4)
    l8 = _each(_bdot, l4, l4)
    dinv = _each(lambda a, b: a + _bdot(a, b), p, l8)
    nm = _each(_bdot, dinv, e)
    n2 = _each(_bdot, nm, nm)
    a1 = _each(lambda a, b: a + _bdot(b, a), dinv, nm)
    return _each(lambda a, b: a + _bdot(b, a), a1, n2)


def _scan_compute(r_ref, k_ref, v_ref, lw_ref, kk_ref, a_ref, g_ref, rk_ref, lnw_ref, lnb_ref,
                  y_ref, s_ref, *, tt, npair):
    c = CHUNK
    nch = tt // c

    @pl.when(pl.program_id(1) == 0)
    def _():
        s_ref[...] = jnp.zeros_like(s_ref)

    row = lax.broadcasted_iota(jnp.int32, (c, LANES), 0)
    lane = lax.broadcasted_iota(jnp.int32, (c, LANES), 1)
    is_a = lane < HEAD_DIM
    s_idx = lane % HEAD_DIM
    strict = s_idx < row
    incl = s_idx <= row
    strict_a = is_a & strict
    strict_b = (~is_a) & strict
    r2 = lax.broadcasted_iota(jnp.int32, (2 * c, LANES), 0)
    l2 = lax.broadcasted_iota(jnp.int32, (2 * c, LANES), 1)
    d16 = (r2 // 16) == (l2 // 16)
    same_head = (r2 // HEAD_DIM) == (l2 // HEAD_DIM)
    tri_incl = (lax.broadcasted_iota(jnp.int32, (c, c), 1)
                <= lax.broadcasted_iota(jnp.int32, (c, c), 0)).astype(BF16)
    gsum = _head_group_matrix(1.0)

    def head_stack(x):
        return jnp.concatenate([jnp.where(is_a, x, 0.0), jnp.where(is_a, 0.0, x)], axis=0)

    def head_stack_swapped(x):
        return jnp.concatenate([jnp.where(is_a, 0.0, x), jnp.where(is_a, x, 0.0)], axis=0)

    elems = [(ci, p) for ci in range(nch) for p in range(npair)]

    def tiles(ref):
        return [ref[ci * c:(ci + 1) * c, p * LANES:(p + 1) * LANES].astype(F32)
                for ci, p in elems]

    kkr = tiles(kk_ref)
    lw = tiles(lw_ref)
    ss = [_bdot(x * x, gsum) for x in kkr]
    c_in = [_split2_dot(tri_incl, x) for x in lw]
    kk = _each(lambda x, s: x * lax.rsqrt(jnp.maximum(s, 1e-24)), kkr, ss)
    b = _each(lambda x, a: x * a, kk, tiles(a_ref))
    e_tot = [jnp.exp(x[c - 1:c, :]) for x in c_in]
    e_neg = [jnp.exp(-x) for x in c_in]
    at = _each(lambda x, ci_, l: -x * jnp.exp(ci_ - l), kk, c_in, lw)
    rt = _each(lambda x, ci_: x * jnp.exp(ci_), tiles(r_ref), c_in)
    bt = _each(lambda x, en: x * en, b, e_neg)
    kt = _each(lambda x, en: x * en, tiles(k_ref), e_neg)
    v = tiles(v_ref)
    ar = _each(lambda x, y: jnp.concatenate([x, y], axis=0).astype(BF16), at, rt)
    bkt = _each(lambda x, y: jnp.concatenate([x, y], axis=0).astype(BF16), bt, kt)
    lhs = [jnp.concatenate([jnp.where(l2 < HEAD_DIM, x, jnp.zeros_like(x)),
                            jnp.where(l2 < HEAD_DIM, jnp.zeros_like(x), x)], axis=0) for x in ar]
    g_ab = _each(lambda x, y: lax.dot_general(x, y, NT_DIMS, preferred_element_type=F32),
                 lhs, bkt)
    g_a = [x[0:2 * c] for x in g_ab]
    g_b = [pltpu.roll(x[2 * c:4 * c], HEAD_DIM, axis=1) for x in g_ab]
    lps = _each(lambda ga, gb: jnp.concatenate([jnp.where(strict_a, ga[0:c], 0.0),
                                                jnp.where(strict_b, gb[0:c], 0.0)], axis=0),
                g_a, g_b)
    lmk = _each(lambda ga, gb: jnp.concatenate(
        [jnp.where(strict, jnp.where(is_a, gb[0:c], ga[0:c]), 0.0),
         jnp.where(incl, jnp.where(is_a, gb[c:2 * c], ga[c:2 * c]), 0.0)], axis=0), g_a, g_b)
    mrb = _each(lambda ga, gb: jnp.where(incl, jnp.where(is_a, ga[c:2 * c], gb[c:2 * c]),
                                         0.0).astype(BF16), g_a, g_b)
    lv = _each(lambda m, x: _bdot(m, head_stack_swapped(x)), lmk, v)
    bk_h = _each(lambda x, y, et: jnp.concatenate([x * et, y * et], axis=0).astype(BF16),
                 bt, kt, e_tot)
    tps = [x.astype(BF16) for x in _unit_lower_inverses(lps, d16)]

    s = [s_ref[p] for p in range(npair)]
    ys = []
    for ci in range(nch):
        idx = [ci * npair + p for p in range(npair)]
        ah = [lax.dot_general(ar[i], s[p].astype(BF16), NT_DIMS, preferred_element_type=F32)
              for p, i in enumerate(idx)]
        x = [ah[p][0:c] + lv[i][0:c] for p, i in enumerate(idx)]
        tu = [jnp.dot(tps[i], head_stack(x[p]).astype(BF16), preferred_element_type=F32)
              for p, i in enumerate(idx)]
        u = [t[0:c] + t[c:2 * c] for t in tu]
        ys.append([ah[p][c:2 * c] + lv[i][c:2 * c]
                   + jnp.dot(mrb[i], head_stack(u[p]).astype(BF16), preferred_element_type=F32)
                   for p, i in enumerate(idx)])
        uv_t = [jnp.concatenate([u[p], v[i]], axis=0).T.astype(BF16)
                for p, i in enumerate(idx)]
        s = [s[p] * e_tot[i]
             + jnp.where(same_head, jnp.dot(uv_t[p], bk_h[i], preferred_element_type=F32), 0.0)
             for p, i in enumerate(idx)]
    for p in range(npair):
        s_ref[p] = s[p]

    gavg = _head_group_matrix(1.0 / HEAD_DIM)
    for p in range(npair):
        cols = slice(p * LANES, (p + 1) * LANES)
        y = jnp.concatenate([ys[ci][p] for ci in range(nch)], axis=0)
        yc = y - _bdot(y, gavg)
        var = _bdot(yc * yc, gavg)
        yn = yc * lax.rsqrt(var + LNX_EPS) * lnw_ref[:, cols] + lnb_ref[:, cols]
        rk = r_ref[:, cols].astype(F32) * k_ref[:, cols].astype(F32) * rk_ref[:, cols]
        bonus = _bdot(rk, gsum) * v_ref[:, cols].astype(F32)
        y_ref[:, cols] = ((yn + bonus) * g_ref[:, cols].astype(F32)).astype(y_ref.dtype)


def _rwkv_mix_kernel(*refs, tt, npair, has_vres):
    n_prep = 16 if has_vres else 13
    prep_in = refs[0:n_prep]
    rk_ref, lnw_ref, lnb_ref = refs[n_prep:n_prep + 3]
    n_out = 1 if has_vres else 2
    outs = refs[n_prep + 3:n_prep + 3 + n_out]
    s_ref, ca_ref, cc_ref, cb_ref = refs[n_prep + 3 + n_out:n_prep + 7 + n_out]
    work = refs[n_prep + 7 + n_out:]
    _prep_compute(prep_in, work, (ca_ref, cc_ref, cb_ref), has_vres=has_vres, tt=tt)
    if not has_vres:
        outs[1][...] = work[2][...].astype(outs[1].dtype)
    _scan_compute(*work, rk_ref, lnw_ref, lnb_ref, outs[0], s_ref, tt=tt, npair=npair)


def _rwkv_mix(proj, vfirst, lp, r_k, lnx_w, lnx_b, *, bsz, seq, tt):
    n = bsz * seq
    nt = seq // tt
    has_vres = vfirst is not None
    d = D_RWKV
    npair = d // LANES
    row = lambda b, t: (b * nt + t, 0)
    const = lambda b, t: (0, 0)
    rowblk = pl.BlockSpec((tt, d), row)
    par = pl.BlockSpec((1, d), const)
    in_specs = [
        pl.BlockSpec((tt, 3 * d), lambda b, t: (b * nt + t, SEG_A // (3 * d))),
        pl.BlockSpec((tt, 256), lambda b, t: (b * nt + t, SEG_C // 256)),
        pl.BlockSpec((tt, 128), lambda b, t: (b * nt + t, SEG_B // 128)),
        pl.BlockSpec((1, 3 * d), const),
        pl.BlockSpec((1, 256), const),
        pl.BlockSpec((1, 128), const),
        par,
        pl.BlockSpec((128, d), const),
        par,
        pl.BlockSpec((128, d), const),
        pl.BlockSpec((256, d), const),
        par,
        par,
    ]
    args = [proj, proj, proj, lp["mu_a"], lp["mu_c"], lp["mu_b"], lp["w0"], lp["wup"], lp["a0"],
            lp["aup"], lp["gup"], lp["k_k"], lp["k_a"]]
    if has_vres:
        in_specs += [rowblk, par, pl.BlockSpec((256, d), const)]
        args += [vfirst, lp["v0"], lp["vup"]]
    in_specs += [par] * 3
    args += [r_k.reshape(1, d), lnx_w.reshape(1, d), lnx_b.reshape(1, d)]
    n_out = 1 if has_vres else 2
    outs = pl.pallas_call(
        functools.partial(_rwkv_mix_kernel, tt=tt, npair=npair, has_vres=has_vres),
        grid=(bsz, nt),
        in_specs=in_specs,
        out_specs=[rowblk] * n_out,
        out_shape=[jax.ShapeDtypeStruct((n, d), BF16)] * n_out,
        scratch_shapes=[pltpu.VMEM((npair, LANES, LANES), F32),
                        pltpu.VMEM((8, 3 * d), F32), pltpu.VMEM((8, 256), F32),
                        pltpu.VMEM((8, 128), F32)] + [pltpu.VMEM((tt, d), F32)] * 7,
        compiler_params=_cparams(("parallel", "arbitrary")),
        name="rwkv_mix",
    )(*args)
    return (outs[0], None) if has_vres else (outs[0], outs[1])


def _sb_kernel(q_ref, k_ref, v_ref, g_ref, o_ref, vab_ref, *, blk, nblk, ppg, scale):
    i = pl.program_id(2)
    lane = lax.broadcasted_iota(jnp.int32, (blk, LANES), 1)
    is_a = lane < HEAD_DIM
    nh = 2 * ppg
    cols = [slice(p * LANES, (p + 1) * LANES) for p in range(ppg)]

    @pl.when(i == 0)
    def _():
        for p in range(ppg):
            for j in range(nblk):
                vb = v_ref[j * blk:(j + 1) * blk, cols[p]]
                vab_ref[p, j, 0:blk, :] = jnp.where(is_a, vb, jnp.zeros_like(vb))
                vab_ref[p, j, blk:2 * blk, :] = jnp.where(is_a, jnp.zeros_like(vb), vb)

    tq = lax.broadcasted_iota(jnp.int32, (blk, blk), 0)
    sk = lax.broadcasted_iota(jnp.int32, (blk, blk), 1)
    causal = sk < tq
    suffix = (tq > sk).astype(BF16)

    q_heads = []
    for p in range(ppg):
        q = q_ref[:, cols[p]].astype(F32) * (scale * LOG2E)
        q_heads += [jnp.where(is_a, q, 0.0).astype(BF16), jnp.where(is_a, 0.0, q).astype(BF16)]

    def block_steps(js, state, diag):
        carries = list(state[0:nh])
        accs = list(state[nh:nh + ppg])
        kbs = [[k_ref[pl.ds(pl.multiple_of(j * blk, blk), blk), cols[p]] for p in range(ppg)]
               for j in js]
        z2 = [[lax.dot_general(q_heads[h], kb[h // 2], NT_DIMS, preferred_element_type=F32)
               for h in range(nh)] for kb in kbs]
        sp = [[_softplus2(z) for z in zs] for zs in z2]
        if diag:
            sp[0] = [jnp.where(causal, s, 0.0) for s in sp[0]]
        spb = [[s.astype(BF16) for s in ss] for ss in sp]
        lsig = [_each(lambda z, s: z - s, zs, ss) for zs, ss in zip(z2, sp)]
        between = [[jnp.dot(s, suffix, preferred_element_type=F32) for s in ss] for ss in spb]
        for b, j in enumerate(js):
            atts = []
            for h in range(nh):
                att = jnp.exp2((lsig[b][h] - between[b][h] - carries[h]).astype(BF16))
                if diag and b == 0:
                    att = jnp.where(causal, att, jnp.zeros_like(att))
                atts.append(att)
                carries[h] = (carries[h] + between[b][h][:, 0:1]
                              + spb[b][h][:, 0:1].astype(F32))
            for p in range(ppg):
                accs[p] = accs[p] + jnp.dot(jnp.concatenate(atts[2 * p:2 * p + 2], axis=1),
                                            vab_ref[p, j], preferred_element_type=F32)
        return tuple(carries) + tuple(accs)

    state = (jnp.zeros((blk, 1), F32),) * nh + (jnp.zeros((blk, LANES), F32),) * ppg
    first = jnp.where(i == 0, 0, 2 - i % 2)
    state = lax.switch(first, [lambda s: block_steps([i], s, True),
                               lambda s: block_steps([i, i - 1], s, True),
                               lambda s: block_steps([i, i - 1, i - 2], s, True)], state)

    def body(m, s):
        j = i - 1 - first - 2 * m
        return block_steps([j, j - 1], s, False)

    state = lax.fori_loop(0, (i - first) // 2, body, state)

    gavg = _head_group_matrix(1.0 / HEAD_DIM)
    for p in range(ppg):
        acc = state[nh + p]
        ms = _bdot(acc * acc, gavg)
        o_ref[:, cols[p]] = (acc * lax.rsqrt(ms + NORM_EPS) * g_ref[:, cols[p]]).astype(o_ref.dtype)


def _sb_attention(qkv, gain, *, bsz, seq, blk, ppg):
    n = bsz * seq
    nq = seq // blk
    ngrp = D_SB // (LANES * ppg)
    w = LANES * ppg
    return pl.pallas_call(
        functools.partial(_sb_kernel, blk=blk, nblk=nq, ppg=ppg, scale=1.0 / (HEAD_DIM ** 0.5)),
        grid=(bsz, ngrp, nq),
        in_specs=[
            pl.BlockSpec((blk, w), lambda b, h, i: (b * nq + i, h)),
            pl.BlockSpec((seq, w), lambda b, h, i: (b, ngrp + h)),
            pl.BlockSpec((seq, w), lambda b, h, i: (b, 2 * ngrp + h)),
            pl.BlockSpec((1, w), lambda b, h, i: (0, h)),
        ],
        out_specs=pl.BlockSpec((blk, w), lambda b, h, i: (b * nq + i, h)),
        out_shape=jax.ShapeDtypeStruct((n, D_SB), BF16),
        scratch_shapes=[pltpu.VMEM((ppg, nq, 2 * blk, LANES), BF16)],
        compiler_params=_cparams(("parallel", "parallel", "arbitrary")),
        name="sb_attention",
    )(qkv, qkv, qkv, gain.reshape(1, D_SB))


def _layer_params(l, d_model, w_in, w_in_vres, mu, mu_vres, w0, w_up, a0, a_up, g_up, v0, v_up,
                  k_k, k_a):
    d = D_RWKV
    wl = w_in[l]
    c_dw = 3 * d
    c_da = c_dw + W_LORA
    c_dg = c_da + A_LORA
    c_sb = c_dg + G_LORA
    zcol = lambda m: jnp.zeros((d_model, m), F32)
    zrow = lambda m: jnp.zeros((m,), F32)
    if l > 0:
        vres_w, vres_mu = w_in_vres[l - 1], mu_vres[l - 1]
    else:
        vres_w, vres_mu = zcol(V_LORA), zrow(V_LORA)
    w_rwkv = jnp.concatenate([
        wl[:, 0:c_dw],
        wl[:, c_dg:c_sb], vres_w, zcol(256 - G_LORA - V_LORA),
        wl[:, c_dw:c_dg],
        zcol(P_RWKV - SEG_B - 128),
    ], axis=1).astype(BF16)
    mul = mu[l]
    lp = {
        "w_rwkv": w_rwkv,
        "w_sb": wl[:, c_sb:].astype(BF16),
        "mu_a": mul[0:c_dw].reshape(1, -1),
        "mu_c": jnp.concatenate([mul[c_dg:c_sb], vres_mu,
                                 zrow(256 - G_LORA - V_LORA)]).reshape(1, -1),
        "mu_b": mul[c_dw:c_dg].reshape(1, -1),
        "w0": w0[l].reshape(1, d),
        "a0": a0[l].reshape(1, d),
        "k_k": k_k[l].reshape(1, d),
        "k_a": k_a[l].reshape(1, d),
        "wup": jnp.concatenate([w_up[l], jnp.zeros((A_LORA, d), F32)], axis=0).astype(BF16),
        "aup": jnp.concatenate([jnp.zeros((W_LORA, d), F32), a_up[l]], axis=0).astype(BF16),
        "gup": jnp.concatenate([g_up[l], jnp.zeros((256 - G_LORA, d), F32)],
                               axis=0).astype(BF16),
    }
    if l > 0:
        lp["v0"] = v0[l - 1].reshape(1, d)
        lp["vup"] = jnp.concatenate([jnp.zeros((G_LORA, d), F32), v_up[l - 1],
                                     jnp.zeros((256 - G_LORA - V_LORA, d), F32)],
                                    axis=0).astype(BF16)
    return lp


def kernel(x, pre_mix_g, post_mix_g, pre_mlp_g, post_mlp_g, w_in, w_in_vres, mu, mu_vres, w0, w_up,
           a0, a_up, g_up, v0, v_up, k_k, k_a, r_k, lnx_w, lnx_b, sb_out_g, w_out, w_ff_up,
           w_ff_down):
    bsz, seq, d_model = x.shape
    depth = w_in.shape[0]
    n = bsz * seq
    assert w_in.shape[2] == 3 * D_RWKV + W_LORA + A_LORA + G_LORA + 3 * D_SB
    tm_mm = min(1024, n)
    tm_res = min(512, n)
    xf = x.reshape(n, d_model)
    h = _norm_cast(xf, pre_mix_g[0], tm=tm_res)
    vfirst = None
    for l in range(depth):
        lp = _layer_params(l, d_model, w_in, w_in_vres, mu, mu_vres, w0, w_up, a0, a_up, g_up,
                           v0, v_up, k_k, k_a)
        proj = _matmul(h, lp["w_rwkv"], relu2=False, tm=tm_mm, tn=512, name="proj_rwkv")
        qkv = _matmul(h, lp["w_sb"], relu2=False, tm=tm_mm, tn=1024, name="proj_sb")
        y_r, v_l = _rwkv_mix(proj, vfirst, lp, r_k[l], lnx_w[l], lnx_b[l],
                             bsz=bsz, seq=seq, tt=min(128, seq))
        if l == 0:
            vfirst = v_l
        y_s = _sb_attention(qkv, sb_out_g[l], bsz=bsz, seq=seq, blk=min(256, seq), ppg=2)
        xf, h = _mix_out(y_r, y_s, w_out[l].astype(BF16), xf, post_mix_g[l], pre_mlp_g[l],
                         tm=tm_res)
        ff = _matmul(h, w_ff_up[l].astype(BF16), relu2=True, tm=tm_mm, tn=1024, name="ffn_up")
        g_next = pre_mix_g[l + 1] if l + 1 < depth else None
        xf, h = _ffn_down(ff, w_ff_down[l].astype(BF16), xf, post_mlp_g[l], g_next,
                          tm=tm_res, tk=2048)
    return xf.reshape(bsz, seq, d_model)
```

```python
import functools

import jax
import jax.numpy as jnp
from jax import lax
from jax.experimental import pallas as pl
from jax.experimental.pallas import tpu as pltpu

F32 = jnp.float32
BF16 = jnp.bfloat16

HEAD_DIM = 64
LANES = 128
NORM_EPS = 1e-6
LNX_EPS = 64e-5
CHUNK = 64
VMEM_LIMIT = 48 * 1024 * 1024
LOG2E = 1.4426950408889634
DECAY_SCALE = 0.6065306597126334

D_RWKV = 1024
D_SB = 1024
W_LORA, A_LORA, V_LORA, G_LORA = 64, 64, 32, 160
SEG_A = 0
SEG_C = 3 * D_RWKV
SEG_B = SEG_C + 256
P_RWKV = 3584

NT_DIMS = (((1,), (1,)), ((), ()))


def _cparams(sem):
    return pltpu.CompilerParams(dimension_semantics=sem, vmem_limit_bytes=VMEM_LIMIT)


def _bdot(a, b):
    return jnp.dot(a.astype(BF16), b.astype(BF16), preferred_element_type=F32)


def _split2_dot(a_exact, x):
    hi = x.astype(BF16)
    lo = (x - hi.astype(F32)).astype(BF16)
    return (jnp.dot(a_exact, hi, preferred_element_type=F32)
            + jnp.dot(a_exact, lo, preferred_element_type=F32))


def _head_group_matrix(scale):
    i = lax.broadcasted_iota(jnp.int32, (LANES, LANES), 0) // HEAD_DIM
    j = lax.broadcasted_iota(jnp.int32, (LANES, LANES), 1) // HEAD_DIM
    return jnp.where(i == j, scale, 0.0).astype(BF16)


def _sigmoid(x):
    return 0.5 * jnp.tanh(0.5 * x) + 0.5


def _neg_abs(x):
    bits = lax.bitcast_convert_type(x, jnp.uint32) | jnp.uint32(0x80000000)
    return lax.bitcast_convert_type(bits, F32)


def _softplus2(z2):
    return jnp.maximum(z2, 0.0) + jnp.log2(1.0 + jnp.exp2(_neg_abs(z2)))


def _each(fn, *lists):
    return [fn(*xs) for xs in zip(*lists)]


def _rms_scaled(x, g):
    ms = jnp.mean(x * x, axis=-1, keepdims=True)
    return x * lax.rsqrt(ms + NORM_EPS) * g


def _norm_cast_kernel(x_ref, g_ref, h_ref):
    h_ref[...] = _rms_scaled(x_ref[...], g_ref[...]).astype(h_ref.dtype)


def _norm_cast(x, g, *, tm):
    n, d = x.shape
    return pl.pallas_call(
        _norm_cast_kernel,
        grid=(n // tm,),
        in_specs=[pl.BlockSpec((tm, d), lambda i: (i, 0)), pl.BlockSpec((1, d), lambda i: (0, 0))],
        out_specs=pl.BlockSpec((tm, d), lambda i: (i, 0)),
        out_shape=jax.ShapeDtypeStruct((n, d), BF16),
        compiler_params=_cparams(("parallel",)),
        name="norm_cast",
    )(x, g.reshape(1, d))


def _matmul_kernel(a_ref, w_ref, o_ref, *, relu2):
    y = jnp.dot(a_ref[...], w_ref[...], preferred_element_type=F32)
    if relu2:
        y = jnp.square(jnp.maximum(y, 0.0))
    o_ref[...] = y.astype(o_ref.dtype)


def _matmul(a, w, *, relu2, tm, tn, name):
    n, kdim = a.shape
    p = w.shape[1]
    return pl.pallas_call(
        functools.partial(_matmul_kernel, relu2=relu2),
        grid=(n // tm, p // tn),
        in_specs=[
            pl.BlockSpec((tm, kdim), lambda i, j: (i, 0)),
            pl.BlockSpec((kdim, tn), lambda i, j: (0, j)),
        ],
        out_specs=pl.BlockSpec((tm, tn), lambda i, j: (i, j)),
        out_shape=jax.ShapeDtypeStruct((n, p), BF16),
        compiler_params=_cparams(("parallel", "arbitrary")),
        name=name,
    )(a, w)


def _residual_epilogue(acc, x_ref, g_ref, gn_ref, o_ref, h_ref):
    xn = x_ref[...] + _rms_scaled(acc, g_ref[...])
    o_ref[...] = xn
    if h_ref is not None:
        h_ref[...] = _rms_scaled(xn, gn_ref[...]).astype(h_ref.dtype)


def _mix_out_kernel(a1_ref, a2_ref, w1_ref, w2_ref, x_ref, g_ref, gn_ref, o_ref, h_ref):
    acc = jnp.dot(a1_ref[...], w1_ref[...], preferred_element_type=F32)
    acc = acc + jnp.dot(a2_ref[...], w2_ref[...], preferred_element_type=F32)
    _residual_epilogue(acc, x_ref, g_ref, gn_ref, o_ref, h_ref)


def _mix_out(a1, a2, w, x, g, g_next, *, tm):
    n, d = x.shape
    k1 = a1.shape[1]
    k2 = a2.shape[1]
    assert k1 == k2 and w.shape[0] == k1 + k2
    return pl.pallas_call(
        _mix_out_kernel,
        grid=(n // tm,),
        in_specs=[
            pl.BlockSpec((tm, k1), lambda i: (i, 0)),
            pl.BlockSpec((tm, k2), lambda i: (i, 0)),
            pl.BlockSpec((k1, d), lambda i: (0, 0)),
            pl.BlockSpec((k2, d), lambda i: (1, 0)),
            pl.BlockSpec((tm, d), lambda i: (i, 0)),
            pl.BlockSpec((1, d), lambda i: (0, 0)),
            pl.BlockSpec((1, d), lambda i: (0, 0)),
        ],
        out_specs=[pl.BlockSpec((tm, d), lambda i: (i, 0))] * 2,
        out_shape=[jax.ShapeDtypeStruct((n, d), F32), jax.ShapeDtypeStruct((n, d), BF16)],
        compiler_params=_cparams(("parallel",)),
        name="mix_out",
    )(a1, a2, w, w, x, g.reshape(1, d), g_next.reshape(1, d))


def _ffn_down_kernel(*refs, nk, emit_h):
    if emit_h:
        a_ref, w_ref, x_ref, g_ref, gn_ref, o_ref, h_ref = refs
    else:
        a_ref, w_ref, x_ref, g_ref, o_ref = refs
        gn_ref = h_ref = None
    k = pl.program_id(1)
    part = jnp.dot(a_ref[...], w_ref[...], preferred_element_type=F32)

    @pl.when(k == 0)
    def _():
        o_ref[...] = part

    @pl.when(k > 0)
    def _():
        o_ref[...] += part

    @pl.when(k == nk - 1)
    def _():
        _residual_epilogue(o_ref[...], x_ref, g_ref, gn_ref, o_ref, h_ref)


def _ffn_down(a, w, x, g, g_next, *, tm, tk):
    n, d = x.shape
    kdim = a.shape[1]
    nk = kdim // tk
    emit_h = g_next is not None
    row = pl.BlockSpec((tm, d), lambda i, k: (i, 0))
    gain = pl.BlockSpec((1, d), lambda i, k: (0, 0))
    in_specs = [pl.BlockSpec((tm, tk), lambda i, k: (i, k)),
                pl.BlockSpec((tk, d), lambda i, k: (k, 0)), row, gain]
    args = [a, w, x, g.reshape(1, d)]
    out_specs = [row]
    out_shape = [jax.ShapeDtypeStruct((n, d), F32)]
    if emit_h:
        in_specs.append(gain)
        args.append(g_next.reshape(1, d))
        out_specs.append(row)
        out_shape.append(jax.ShapeDtypeStruct((n, d), BF16))
    outs = pl.pallas_call(
        functools.partial(_ffn_down_kernel, nk=nk, emit_h=emit_h),
        grid=(n // tm, nk),
        in_specs=in_specs,
        out_specs=out_specs,
        out_shape=out_shape,
        compiler_params=_cparams(("parallel", "arbitrary")),
        name="ffn_down",
    )(*args)
    return (outs[0], outs[1]) if emit_h else (outs[0], None)


def _prep_compute(in_refs, out_refs, carry_refs, *, has_vres, tt):
    (pa_ref, pc_ref, pb_ref, mua_ref, muc_ref, mub_ref, w0_ref, wup_ref, a0_ref, aup_ref,
     gup_ref, kk_ref, ka_ref) = in_refs[0:13]
    if has_vres:
        vf_ref, v0_ref, vup_ref = in_refs[13:16]
    r_out, k_out, v_out, lw_out, kk_out, a_out, g_out = out_refs
    ca_ref, cc_ref, cb_ref = carry_refs

    @pl.when(pl.program_id(1) == 0)
    def _():
        ca_ref[...] = jnp.zeros_like(ca_ref)
        cc_ref[...] = jnp.zeros_like(cc_ref)
        cb_ref[...] = jnp.zeros_like(cb_ref)

    down = (lax.broadcasted_iota(jnp.int32, (tt, tt), 1) + 1
            == lax.broadcasted_iota(jnp.int32, (tt, tt), 0)).astype(BF16)

    def shifted(pb16, prev_row, mu):
        p = pb16.astype(F32)
        prev = jnp.dot(down, pb16, preferred_element_type=F32)
        row = lax.broadcasted_iota(jnp.int32, p.shape, 0)
        prev = jnp.where(row == 0, prev_row, prev)
        return p + (prev - p) * mu

    xb = shifted(pb_ref[...], cb_ref[7:8, :], mub_ref[...])
    xc = shifted(pc_ref[...], cc_ref[7:8, :], muc_ref[...])

    wl = w0_ref[...] + _bdot(jnp.tanh(xb), wup_ref[...])
    lw_out[...] = -DECAY_SCALE * _sigmoid(wl)
    a = _sigmoid(a0_ref[...] + _bdot(xb, aup_ref[...]))
    a_out[...] = a.astype(a_out.dtype)
    g_out[...] = _bdot(_sigmoid(xc), gup_ref[...]).astype(g_out.dtype)

    d = D_RWKV
    r_out[...] = shifted(pa_ref[:, 0:d], ca_ref[7:8, 0:d], mua_ref[:, 0:d]).astype(r_out.dtype)
    k = shifted(pa_ref[:, d:2 * d], ca_ref[7:8, d:2 * d], mua_ref[:, d:2 * d])
    kk_out[...] = (k * kk_ref[...]).astype(kk_out.dtype)
    k_out[...] = (k * (1.0 + (a - 1.0) * ka_ref[...])).astype(k_out.dtype)
    v = shifted(pa_ref[:, 2 * d:3 * d], ca_ref[7:8, 2 * d:3 * d], mua_ref[:, 2 * d:3 * d])
    if has_vres:
        mix = _sigmoid(v0_ref[...] + _bdot(xc, vup_ref[...]))
        v = v + (vf_ref[...].astype(F32) - v) * mix
    v_out[...] = v.astype(v_out.dtype)

    ca_ref[...] = pa_ref[tt - 8:tt, :].astype(F32)
    cc_ref[...] = pc_ref[tt - 8:tt, :].astype(F32)
    cb_ref[...] = pb_ref[tt - 8:tt, :].astype(F32)


def _unit_lower_inverses(lps, d16):
    n = lps[0].shape[0]
    eye = (lax.broadcasted_iota(jnp.int32, (n, n), 0)
           == lax.broadcasted_iota(jnp.int32, (n, n), 1)).astype(F32)
    ld = [jnp.where(d16, lp, 0.0) for lp in lps]
    e = _each(lambda a, b: a - b, lps, ld)
    l2 = _each(_bdot, ld, ld)
    p = [eye + x for x in ld]
    p = _each(lambda a, b: a + _bdot(a, b), p, l2)
    l4 = _each(_bdot, l2, l2)
    p = _each(lambda a, b: a + _bdot(a, b), p, l4)
    l8 = _each(_bdot, l4, l4)
    dinv = _each(lambda a, b: a + _bdot(a, b), p, l8)
    nm = _each(_bdot, dinv, e)
    n2 = _each(_bdot, nm, nm)
    a1 = _each(lambda a, b: a + _bdot(b, a), dinv, nm)
    return _each(lambda a, b: a + _bdot(b, a), a1, n2)


def _scan_compute(r_ref, k_ref, v_ref, lw_ref, kk_ref, a_ref, g_ref, rk_ref, lnw_ref, lnb_ref,
                  y_ref, s_ref, *, tt, npair):
    c = CHUNK
    nch = tt // c

    @pl.when(pl.program_id(1) == 0)
    def _():
        s_ref[...] = jnp.zeros_like(s_ref)

    row = lax.broadcasted_iota(jnp.int32, (c, LANES), 0)
    lane = lax.broadcasted_iota(jnp.int32, (c, LANES), 1)
    is_a = lane < HEAD_DIM
    s_idx = lane % HEAD_DIM
    strict = s_idx < row
    incl = s_idx <= row
    strict_a = is_a & strict
    strict_b = (~is_a) & strict
    r2 = lax.broadcasted_iota(jnp.int32, (2 * c, LANES), 0)
    l2 = lax.broadcasted_iota(jnp.int32, (2 * c, LANES), 1)
    d16 = (r2 // 16) == (l2 // 16)
    same_head = (r2 // HEAD_DIM) == (l2 // HEAD_DIM)
    tri_incl = (lax.broadcasted_iota(jnp.int32, (c, c), 1)
                <= lax.broadcasted_iota(jnp.int32, (c, c), 0)).astype(BF16)
    gsum = _head_group_matrix(1.0)

    def head_stack(x):
        return jnp.concatenate([jnp.where(is_a, x, 0.0), jnp.where(is_a, 0.0, x)], axis=0)

    def head_stack_swapped(x):
        return jnp.concatenate([jnp.where(is_a, 0.0, x), jnp.where(is_a, x, 0.0)], axis=0)

    elems = [(ci, p) for ci in range(nch) for p in range(npair)]

    def tiles(ref):
        return [ref[ci * c:(ci + 1) * c, p * LANES:(p + 1) * LANES].astype(F32)
                for ci, p in elems]

    kkr = tiles(kk_ref)
    lw = tiles(lw_ref)
    ss = [_bdot(x * x, gsum) for x in kkr]
    c_in = [_split2_dot(tri_incl, x) for x in lw]
    kk = _each(lambda x, s: x * lax.rsqrt(jnp.maximum(s, 1e-24)), kkr, ss)
    b = _each(lambda x, a: x * a, kk, tiles(a_ref))
    e_tot = [jnp.exp(x[c - 1:c, :]) for x in c_in]
    e_neg = [jnp.exp(-x) for x in c_in]
    at = _each(lambda x, ci_, l: -x * jnp.exp(ci_ - l), kk, c_in, lw)
    rt = _each(lambda x, ci_: x * jnp.exp(ci_), tiles(r_ref), c_in)
    bt = _each(lambda x, en: x * en, b, e_neg)
    kt = _each(lambda x, en: x * en, tiles(k_ref), e_neg)
    v = tiles(v_ref)
    ar = _each(lambda x, y: jnp.concatenate([x, y], axis=0).astype(BF16), at, rt)
    bkt = _each(lambda x, y: jnp.concatenate([x, y], axis=0).astype(BF16), bt, kt)
    lhs = [jnp.concatenate([jnp.where(l2 < HEAD_DIM, x, jnp.zeros_like(x)),
                            jnp.where(l2 < HEAD_DIM, jnp.zeros_like(x), x)], axis=0) for x in ar]
    g_ab = _each(lambda x, y: lax.dot_general(x, y, NT_DIMS, preferred_element_type=F32),
                 lhs, bkt)
    g_a = [x[0:2 * c] for x in g_ab]
    g_b = [pltpu.roll(x[2 * c:4 * c], HEAD_DIM, axis=1) for x in g_ab]
    lps = _each(lambda ga, gb: jnp.concatenate([jnp.where(strict_a, ga[0:c], 0.0),
                                                jnp.where(strict_b, gb[0:c], 0.0)], axis=0),
                g_a, g_b)
    lmk = _each(lambda ga, gb: jnp.concatenate(
        [jnp.where(strict, jnp.where(is_a, gb[0:c], ga[0:c]), 0.0),
         jnp.where(incl, jnp.where(is_a, gb[c:2 * c], ga[c:2 * c]), 0.0)], axis=0), g_a, g_b)
    mrb = _each(lambda ga, gb: jnp.where(incl, jnp.where(is_a, ga[c:2 * c], gb[c:2 * c]),
                                         0.0).astype(BF16), g_a, g_b)
    lv = _each(lambda m, x: _bdot(m, head_stack_swapped(x)), lmk, v)
    bk_h = _each(lambda x, y, et: jnp.concatenate([x * et, y * et], axis=0).astype(BF16),
                 bt, kt, e_tot)
    tps = [x.astype(BF16) for x in _unit_lower_inverses(lps, d16)]

    s = [s_ref[p] for p in range(npair)]
    ys = []
    for ci in range(nch):
        idx = [ci * npair + p for p in range(npair)]
        ah = [lax.dot_general(ar[i], s[p].astype(BF16), NT_DIMS, preferred_element_type=F32)
              for p, i in enumerate(idx)]
        x = [ah[p][0:c] + lv[i][0:c] for p, i in enumerate(idx)]
        tu = [jnp.dot(tps[i], head_stack(x[p]).astype(BF16), preferred_element_type=F32)
              for p, i in enumerate(idx)]
        u = [t[0:c] + t[c:2 * c] for t in tu]
        ys.append([ah[p][c:2 * c] + lv[i][c:2 * c]
                   + jnp.dot(mrb[i], head_stack(u[p]).astype(BF16), preferred_element_type=F32)
                   for p, i in enumerate(idx)])
        uv_t = [jnp.concatenate([u[p], v[i]], axis=0).T.astype(BF16)
                for p, i in enumerate(idx)]
        s = [s[p] * e_tot[i]
             + jnp.where(same_head, jnp.dot(uv_t[p], bk_h[i], preferred_element_type=F32), 0.0)
             for p, i in enumerate(idx)]
    for p in range(npair):
        s_ref[p] = s[p]

    gavg = _head_group_matrix(1.0 / HEAD_DIM)
    cols = [slice(p * LANES, (p + 1) * LANES) for p in range(npair)]
    y = [jnp.concatenate([ys[ci][p] for ci in range(nch)], axis=0) for p in range(npair)]
    rk = [r_ref[:, cs].astype(F32) * k_ref[:, cs].astype(F32) * rk_ref[:, cs] for cs in cols]
    mean = [_bdot(x, gavg) for x in y]
    rk_sum = [_bdot(x, gsum) for x in rk]
    yc = _each(lambda a, b: a - b, y, mean)
    var = [_bdot(x * x, gavg) for x in yc]
    for p, cs in enumerate(cols):
        yn = yc[p] * lax.rsqrt(var[p] + LNX_EPS) * lnw_ref[:, cs] + lnb_ref[:, cs]
        bonus = rk_sum[p] * v_ref[:, cs].astype(F32)
        y_ref[:, cs] = ((yn + bonus) * g_ref[:, cs].astype(F32)).astype(y_ref.dtype)


def _rwkv_mix_kernel(*refs, tt, npair, has_vres):
    n_prep = 16 if has_vres else 13
    prep_in = refs[0:n_prep]
    rk_ref, lnw_ref, lnb_ref = refs[n_prep:n_prep + 3]
    n_out = 1 if has_vres else 2
    outs = refs[n_prep + 3:n_prep + 3 + n_out]
    s_ref, ca_ref, cc_ref, cb_ref = refs[n_prep + 3 + n_out:n_prep + 7 + n_out]
    work = refs[n_prep + 7 + n_out:]
    _prep_compute(prep_in, work, (ca_ref, cc_ref, cb_ref), has_vres=has_vres, tt=tt)
    if not has_vres:
        outs[1][...] = work[2][...].astype(outs[1].dtype)
    _scan_compute(*work, rk_ref, lnw_ref, lnb_ref, outs[0], s_ref, tt=tt, npair=npair)


def _rwkv_mix(proj, vfirst, lp, r_k, lnx_w, lnx_b, *, bsz, seq, tt):
    n = bsz * seq
    nt = seq // tt
    has_vres = vfirst is not None
    d = D_RWKV
    npair = d // LANES
    row = lambda b, t: (b * nt + t, 0)
    const = lambda b, t: (0, 0)
    rowblk = pl.BlockSpec((tt, d), row)
    par = pl.BlockSpec((1, d), const)
    in_specs = [
        pl.BlockSpec((tt, 3 * d), lambda b, t: (b * nt + t, SEG_A // (3 * d))),
        pl.BlockSpec((tt, 256), lambda b, t: (b * nt + t, SEG_C // 256)),
        pl.BlockSpec((tt, 128), lambda b, t: (b * nt + t, SEG_B // 128)),
        pl.BlockSpec((1, 3 * d), const),
        pl.BlockSpec((1, 256), const),
        pl.BlockSpec((1, 128), const),
        par,
        pl.BlockSpec((128, d), const),
        par,
        pl.BlockSpec((128, d), const),
        pl.BlockSpec((256, d), const),
        par,
        par,
    ]
    args = [proj, proj, proj, lp["mu_a"], lp["mu_c"], lp["mu_b"], lp["w0"], lp["wup"], lp["a0"],
            lp["aup"], lp["gup"], lp["k_k"], lp["k_a"]]
    if has_vres:
        in_specs += [rowblk, par, pl.BlockSpec((256, d), const)]
        args += [vfirst, lp["v0"], lp["vup"]]
    in_specs += [par] * 3
    args += [r_k.reshape(1, d), lnx_w.reshape(1, d), lnx_b.reshape(1, d)]
    n_out = 1 if has_vres else 2
    outs = pl.pallas_call(
        functools.partial(_rwkv_mix_kernel, tt=tt, npair=npair, has_vres=has_vres),
        grid=(bsz, nt),
        in_specs=in_specs,
        out_specs=[rowblk] * n_out,
        out_shape=[jax.ShapeDtypeStruct((n, d), BF16)] * n_out,
        scratch_shapes=[pltpu.VMEM((npair, LANES, LANES), F32),
                        pltpu.VMEM((8, 3 * d), F32), pltpu.VMEM((8, 256), F32),
                        pltpu.VMEM((8, 128), F32)] + [pltpu.VMEM((tt, d), F32)] * 7,
        compiler_params=_cparams(("parallel", "arbitrary")),
        name="rwkv_mix",
    )(*args)
    return (outs[0], None) if has_vres else (outs[0], outs[1])


def _sb_kernel(q_ref, k_ref, v_ref, g_ref, o_ref, vab_ref, *, blk, nblk, ppg, scale):
    i = pl.program_id(2)
    lane = lax.broadcasted_iota(jnp.int32, (blk, LANES), 1)
    is_a = lane < HEAD_DIM
    nh = 2 * ppg
    cols = [slice(p * LANES, (p + 1) * LANES) for p in range(ppg)]

    @pl.when(i == 0)
    def _():
        for p in range(ppg):
            for j in range(nblk):
                vb = v_ref[j * blk:(j + 1) * blk, cols[p]]
                vab_ref[p, j, 0:blk, :] = jnp.where(is_a, vb, jnp.zeros_like(vb))
                vab_ref[p, j, blk:2 * blk, :] = jnp.where(is_a, jnp.zeros_like(vb), vb)

    tq = lax.broadcasted_iota(jnp.int32, (blk, blk), 0)
    sk = lax.broadcasted_iota(jnp.int32, (blk, blk), 1)
    causal = sk < tq
    suffix = (tq > sk).astype(BF16)

    q_heads = []
    for p in range(ppg):
        q = q_ref[:, cols[p]].astype(F32) * (scale * LOG2E)
        q_heads += [jnp.where(is_a, q, 0.0).astype(BF16), jnp.where(is_a, 0.0, q).astype(BF16)]

    def block_steps(js, state, diag):
        carries = list(state[0:nh])
        accs = list(state[nh:nh + ppg])
        kbs = [[k_ref[pl.ds(pl.multiple_of(j * blk, blk), blk), cols[p]] for p in range(ppg)]
               for j in js]
        z2 = [[lax.dot_general(q_heads[h], kb[h // 2], NT_DIMS, preferred_element_type=F32)
               for h in range(nh)] for kb in kbs]
        sp = [[_softplus2(z) for z in zs] for zs in z2]
        if diag:
            sp[0] = [jnp.where(causal, s, 0.0) for s in sp[0]]
        spb = [[s.astype(BF16) for s in ss] for ss in sp]
        lsig = [_each(lambda z, s: z - s, zs, ss) for zs, ss in zip(z2, sp)]
        between = [[jnp.dot(s, suffix, preferred_element_type=F32) for s in ss] for ss in spb]
        for b, j in enumerate(js):
            atts = []
            for h in range(nh):
                att = jnp.exp2((lsig[b][h] - between[b][h] - carries[h]).astype(BF16))
                if diag and b == 0:
                    att = jnp.where(causal, att, jnp.zeros_like(att))
                atts.append(att)
                carries[h] = (carries[h] + between[b][h][:, 0:1]
                              + spb[b][h][:, 0:1].astype(F32))
            for p in range(ppg):
                accs[p] = accs[p] + jnp.dot(jnp.concatenate(atts[2 * p:2 * p + 2], axis=1),
                                            vab_ref[p, j], preferred_element_type=F32)
        return tuple(carries) + tuple(accs)

    state = (jnp.zeros((blk, 1), F32),) * nh + (jnp.zeros((blk, LANES), F32),) * ppg
    first = jnp.where(i == 0, 0, 2 - i % 2)
    state = lax.switch(first, [lambda s: block_steps([i], s, True),
                               lambda s: block_steps([i, i - 1], s, True),
                               lambda s: block_steps([i, i - 1, i - 2], s, True)], state)

    def body(m, s):
        j = i - 1 - first - 2 * m
        return block_steps([j, j - 1], s, False)

    state = lax.fori_loop(0, (i - first) // 2, body, state)

    gavg = _head_group_matrix(1.0 / HEAD_DIM)
    for p in range(ppg):
        acc = state[nh + p]
        ms = _bdot(acc * acc, gavg)
        o_ref[:, cols[p]] = (acc * lax.rsqrt(ms + NORM_EPS) * g_ref[:, cols[p]]).astype(o_ref.dtype)


def _sb_attention(qkv, gain, *, bsz, seq, blk, ppg):
    n = bsz * seq
    nq = seq // blk
    ngrp = D_SB // (LANES * ppg)
    w = LANES * ppg
    return pl.pallas_call(
        functools.partial(_sb_kernel, blk=blk, nblk=nq, ppg=ppg, scale=1.0 / (HEAD_DIM ** 0.5)),
        grid=(bsz, ngrp, nq),
        in_specs=[
            pl.BlockSpec((blk, w), lambda b, h, i: (b * nq + i, h)),
            pl.BlockSpec((seq, w), lambda b, h, i: (b, ngrp + h)),
            pl.BlockSpec((seq, w), lambda b, h, i: (b, 2 * ngrp + h)),
            pl.BlockSpec((1, w), lambda b, h, i: (0, h)),
        ],
        out_specs=pl.BlockSpec((blk, w), lambda b, h, i: (b * nq + i, h)),
        out_shape=jax.ShapeDtypeStruct((n, D_SB), BF16),
        scratch_shapes=[pltpu.VMEM((ppg, nq, 2 * blk, LANES), BF16)],
        compiler_params=_cparams(("parallel", "parallel", "arbitrary")),
        name="sb_attention",
    )(qkv, qkv, qkv, gain.reshape(1, D_SB))


def _layer_params(l, d_model, w_in, w_in_vres, mu, mu_vres, w0, w_up, a0, a_up, g_up, v0, v_up,
                  k_k, k_a):
    d = D_RWKV
    wl = w_in[l]
    c_dw = 3 * d
    c_da = c_dw + W_LORA
    c_dg = c_da + A_LORA
    c_sb = c_dg + G_LORA
    zcol = lambda m: jnp.zeros((d_model, m), F32)
    zrow = lambda m: jnp.zeros((m,), F32)
    if l > 0:
        vres_w, vres_mu = w_in_vres[l - 1], mu_vres[l - 1]
    else:
        vres_w, vres_mu = zcol(V_LORA), zrow(V_LORA)
    w_rwkv = jnp.concatenate([
        wl[:, 0:c_dw],
        wl[:, c_dg:c_sb], vres_w, zcol(256 - G_LORA - V_LORA),
        wl[:, c_dw:c_dg],
        zcol(P_RWKV - SEG_B - 128),
    ], axis=1).astype(BF16)
    mul = mu[l]
    lp = {
        "w_rwkv": w_rwkv,
        "w_sb": wl[:, c_sb:].astype(BF16),
        "mu_a": mul[0:c_dw].reshape(1, -1),
        "mu_c": jnp.concatenate([mul[c_dg:c_sb], vres_mu,
                                 zrow(256 - G_LORA - V_LORA)]).reshape(1, -1),
        "mu_b": mul[c_dw:c_dg].reshape(1, -1),
        "w0": w0[l].reshape(1, d),
        "a0": a0[l].reshape(1, d),
        "k_k": k_k[l].reshape(1, d),
        "k_a": k_a[l].reshape(1, d),
        "wup": jnp.concatenate([w_up[l], jnp.zeros((A_LORA, d), F32)], axis=0).astype(BF16),
        "aup": jnp.concatenate([jnp.zeros((W_LORA, d), F32), a_up[l]], axis=0).astype(BF16),
        "gup": jnp.concatenate([g_up[l], jnp.zeros((256 - G_LORA, d), F32)],
                               axis=0).astype(BF16),
    }
    if l > 0:
        lp["v0"] = v0[l - 1].reshape(1, d)
        lp["vup"] = jnp.concatenate([jnp.zeros((G_LORA, d), F32), v_up[l - 1],
                                     jnp.zeros((256 - G_LORA - V_LORA, d), F32)],
                                    axis=0).astype(BF16)
    return lp


def kernel(x, pre_mix_g, post_mix_g, pre_mlp_g, post_mlp_g, w_in, w_in_vres, mu, mu_vres, w0, w_up,
           a0, a_up, g_up, v0, v_up, k_k, k_a, r_k, lnx_w, lnx_b, sb_out_g, w_out, w_ff_up,
           w_ff_down):
    bsz, seq, d_model = x.shape
    depth = w_in.shape[0]
    n = bsz * seq
    assert w_in.shape[2] == 3 * D_RWKV + W_LORA + A_LORA + G_LORA + 3 * D_SB
    tm_mm = min(1024, n)
    tm_res = min(512, n)
    xf = x.reshape(n, d_model)
    h = _norm_cast(xf, pre_mix_g[0], tm=tm_res)
    vfirst = None
    for l in range(depth):
        lp = _layer_params(l, d_model, w_in, w_in_vres, mu, mu_vres, w0, w_up, a0, a_up, g_up,
                           v0, v_up, k_k, k_a)
        proj = _matmul(h, lp["w_rwkv"], relu2=False, tm=tm_mm, tn=P_RWKV // 2, name="proj_rwkv")
        qkv = _matmul(h, lp["w_sb"], relu2=False, tm=tm_mm, tn=1536, name="proj_sb")
        y_r, v_l = _rwkv_mix(proj, vfirst, lp, r_k[l], lnx_w[l], lnx_b[l],
                             bsz=bsz, seq=seq, tt=min(128, seq))
        if l == 0:
            vfirst = v_l
        y_s = _sb_attention(qkv, sb_out_g[l], bsz=bsz, seq=seq, blk=min(256, seq), ppg=2)
        xf, h = _mix_out(y_r, y_s, w_out[l].astype(BF16), xf, post_mix_g[l], pre_mlp_g[l],
                         tm=tm_res)
        ff = _matmul(h, w_ff_up[l].astype(BF16), relu2=True, tm=tm_mm, tn=2048, name="ffn_up")
        g_next = pre_mix_g[l + 1] if l + 1 < depth else None
        xf, h = _ffn_down(ff, w_ff_down[l].astype(BF16), xf, post_mlp_g[l], g_next,
                          tm=tm_res, tk=2048)
    return xf.reshape(bsz, seq, d_model)
```

```python
import functools

import jax
import jax.numpy as jnp
from jax import lax
from jax.experimental import pallas as pl
from jax.experimental.pallas import tpu as pltpu

F32 = jnp.float32
BF16 = jnp.bfloat16

HEAD_DIM = 64
LANES = 128
NORM_EPS = 1e-6
LNX_EPS = 64e-5
CHUNK = 64
VMEM_LIMIT = 48 * 1024 * 1024
LOG2E = 1.4426950408889634
DECAY_SCALE = 0.6065306597126334

D_RWKV = 1024
D_SB = 1024
W_LORA, A_LORA, V_LORA, G_LORA = 64, 64, 32, 160
SEG_A = 0
SEG_C = 3 * D_RWKV
SEG_B = SEG_C + 256
P_RWKV = 3584

NT_DIMS = (((1,), (1,)), ((), ()))


def _cparams(sem):
    return pltpu.CompilerParams(dimension_semantics=sem, vmem_limit_bytes=VMEM_LIMIT)


def _bdot(a, b):
    return jnp.dot(a.astype(BF16), b.astype(BF16), preferred_element_type=F32)


def _split2_dot(a_exact, x):
    hi = x.astype(BF16)
    lo = (x - hi.astype(F32)).astype(BF16)
    return (jnp.dot(a_exact, hi, preferred_element_type=F32)
            + jnp.dot(a_exact, lo, preferred_element_type=F32))


def _head_group_matrix(scale):
    i = lax.broadcasted_iota(jnp.int32, (LANES, LANES), 0) // HEAD_DIM
    j = lax.broadcasted_iota(jnp.int32, (LANES, LANES), 1) // HEAD_DIM
    return jnp.where(i == j, scale, 0.0).astype(BF16)


def _sigmoid(x):
    return 0.5 * jnp.tanh(0.5 * x) + 0.5


def _neg_abs(x):
    bits = lax.bitcast_convert_type(x, jnp.uint32) | jnp.uint32(0x80000000)
    return lax.bitcast_convert_type(bits, F32)


def _softplus2(z2):
    return jnp.maximum(z2, 0.0) + jnp.log2(1.0 + jnp.exp2(_neg_abs(z2)))


def _stacked_dot(xs, w):
    rows = xs[0].shape[0]
    out = _bdot(jnp.concatenate(xs, axis=0), w)
    return [out[i * rows:(i + 1) * rows] for i in range(len(xs))]


def _each(fn, *lists):
    return [fn(*xs) for xs in zip(*lists)]


def _rms_scaled(x, g):
    ms = jnp.mean(x * x, axis=-1, keepdims=True)
    return x * lax.rsqrt(ms + NORM_EPS) * g


def _norm_cast_kernel(x_ref, g_ref, h_ref):
    h_ref[...] = _rms_scaled(x_ref[...], g_ref[...]).astype(h_ref.dtype)


def _norm_cast(x, g, *, tm):
    n, d = x.shape
    return pl.pallas_call(
        _norm_cast_kernel,
        grid=(n // tm,),
        in_specs=[pl.BlockSpec((tm, d), lambda i: (i, 0)), pl.BlockSpec((1, d), lambda i: (0, 0))],
        out_specs=pl.BlockSpec((tm, d), lambda i: (i, 0)),
        out_shape=jax.ShapeDtypeStruct((n, d), BF16),
        compiler_params=_cparams(("parallel",)),
        name="norm_cast",
    )(x, g.reshape(1, d))


def _matmul_kernel(a_ref, w_ref, o_ref, *, relu2):
    y = jnp.dot(a_ref[...], w_ref[...], preferred_element_type=F32)
    if relu2:
        y = jnp.square(jnp.maximum(y, 0.0))
    o_ref[...] = y.astype(o_ref.dtype)


def _matmul(a, w, *, relu2, tm, tn, name):
    n, kdim = a.shape
    p = w.shape[1]
    return pl.pallas_call(
        functools.partial(_matmul_kernel, relu2=relu2),
        grid=(n // tm, p // tn),
        in_specs=[
            pl.BlockSpec((tm, kdim), lambda i, j: (i, 0)),
            pl.BlockSpec((kdim, tn), lambda i, j: (0, j)),
        ],
        out_specs=pl.BlockSpec((tm, tn), lambda i, j: (i, j)),
        out_shape=jax.ShapeDtypeStruct((n, p), BF16),
        compiler_params=_cparams(("parallel", "arbitrary")),
        name=name,
    )(a, w)


def _residual_epilogue(acc, x_ref, g_ref, gn_ref, o_ref, h_ref):
    xn = x_ref[...] + _rms_scaled(acc, g_ref[...])
    o_ref[...] = xn
    if h_ref is not None:
        h_ref[...] = _rms_scaled(xn, gn_ref[...]).astype(h_ref.dtype)


def _mix_out_kernel(a1_ref, a2_ref, w1_ref, w2_ref, x_ref, g_ref, gn_ref, o_ref, h_ref):
    acc = jnp.dot(a1_ref[...], w1_ref[...], preferred_element_type=F32)
    acc = acc + jnp.dot(a2_ref[...], w2_ref[...], preferred_element_type=F32)
    _residual_epilogue(acc, x_ref, g_ref, gn_ref, o_ref, h_ref)


def _mix_out(a1, a2, w, x, g, g_next, *, tm):
    n, d = x.shape
    k1 = a1.shape[1]
    k2 = a2.shape[1]
    assert k1 == k2 and w.shape[0] == k1 + k2
    return pl.pallas_call(
        _mix_out_kernel,
        grid=(n // tm,),
        in_specs=[
            pl.BlockSpec((tm, k1), lambda i: (i, 0)),
            pl.BlockSpec((tm, k2), lambda i: (i, 0)),
            pl.BlockSpec((k1, d), lambda i: (0, 0)),
            pl.BlockSpec((k2, d), lambda i: (1, 0)),
            pl.BlockSpec((tm, d), lambda i: (i, 0)),
            pl.BlockSpec((1, d), lambda i: (0, 0)),
            pl.BlockSpec((1, d), lambda i: (0, 0)),
        ],
        out_specs=[pl.BlockSpec((tm, d), lambda i: (i, 0))] * 2,
        out_shape=[jax.ShapeDtypeStruct((n, d), F32), jax.ShapeDtypeStruct((n, d), BF16)],
        compiler_params=_cparams(("parallel",)),
        name="mix_out",
    )(a1, a2, w, w, x, g.reshape(1, d), g_next.reshape(1, d))


def _ffn_down_kernel(*refs, nk, emit_h):
    if emit_h:
        a_ref, w_ref, x_ref, g_ref, gn_ref, o_ref, h_ref = refs
    else:
        a_ref, w_ref, x_ref, g_ref, o_ref = refs
        gn_ref = h_ref = None
    k = pl.program_id(1)
    part = jnp.dot(a_ref[...], w_ref[...], preferred_element_type=F32)

    @pl.when(k == 0)
    def _():
        o_ref[...] = part

    @pl.when(k > 0)
    def _():
        o_ref[...] += part

    @pl.when(k == nk - 1)
    def _():
        _residual_epilogue(o_ref[...], x_ref, g_ref, gn_ref, o_ref, h_ref)


def _ffn_down(a, w, x, g, g_next, *, tm, tk):
    n, d = x.shape
    kdim = a.shape[1]
    nk = kdim // tk
    emit_h = g_next is not None
    row = pl.BlockSpec((tm, d), lambda i, k: (i, 0))
    gain = pl.BlockSpec((1, d), lambda i, k: (0, 0))
    in_specs = [pl.BlockSpec((tm, tk), lambda i, k: (i, k)),
                pl.BlockSpec((tk, d), lambda i, k: (k, 0)), row, gain]
    args = [a, w, x, g.reshape(1, d)]
    out_specs = [row]
    out_shape = [jax.ShapeDtypeStruct((n, d), F32)]
    if emit_h:
        in_specs.append(gain)
        args.append(g_next.reshape(1, d))
        out_specs.append(row)
        out_shape.append(jax.ShapeDtypeStruct((n, d), BF16))
    outs = pl.pallas_call(
        functools.partial(_ffn_down_kernel, nk=nk, emit_h=emit_h),
        grid=(n // tm, nk),
        in_specs=in_specs,
        out_specs=out_specs,
        out_shape=out_shape,
        compiler_params=_cparams(("parallel", "arbitrary")),
        name="ffn_down",
    )(*args)
    return (outs[0], outs[1]) if emit_h else (outs[0], None)


def _prep_compute(in_refs, out_refs, carry_refs, *, has_vres, tt):
    (pa_ref, pc_ref, pb_ref, mua_ref, muc_ref, mub_ref, w0_ref, wup_ref, a0_ref, aup_ref,
     gup_ref, kk_ref, ka_ref) = in_refs[0:13]
    if has_vres:
        vf_ref, v0_ref, vup_ref = in_refs[13:16]
    r_out, k_out, v_out, lw_out, kk_out, a_out, g_out = out_refs
    ca_ref, cc_ref, cb_ref = carry_refs

    @pl.when(pl.program_id(1) == 0)
    def _():
        ca_ref[...] = jnp.zeros_like(ca_ref)
        cc_ref[...] = jnp.zeros_like(cc_ref)
        cb_ref[...] = jnp.zeros_like(cb_ref)

    down = (lax.broadcasted_iota(jnp.int32, (tt, tt), 1) + 1
            == lax.broadcasted_iota(jnp.int32, (tt, tt), 0)).astype(BF16)

    def shifted(pb16, prev_row, mu):
        p = pb16.astype(F32)
        prev = jnp.dot(down, pb16, preferred_element_type=F32)
        row = lax.broadcasted_iota(jnp.int32, p.shape, 0)
        prev = jnp.where(row == 0, prev_row, prev)
        return p + (prev - p) * mu

    xb = shifted(pb_ref[...], cb_ref[7:8, :], mub_ref[...])
    xc = shifted(pc_ref[...], cc_ref[7:8, :], muc_ref[...])

    wl = w0_ref[...] + _bdot(jnp.tanh(xb), wup_ref[...])
    lw_out[...] = -DECAY_SCALE * _sigmoid(wl)
    a = _sigmoid(a0_ref[...] + _bdot(xb, aup_ref[...]))
    a_out[...] = a.astype(a_out.dtype)
    g_out[...] = _bdot(_sigmoid(xc), gup_ref[...]).astype(g_out.dtype)

    d = D_RWKV
    r_out[...] = shifted(pa_ref[:, 0:d], ca_ref[7:8, 0:d], mua_ref[:, 0:d]).astype(r_out.dtype)
    k = shifted(pa_ref[:, d:2 * d], ca_ref[7:8, d:2 * d], mua_ref[:, d:2 * d])
    kk_out[...] = (k * kk_ref[...]).astype(kk_out.dtype)
    k_out[...] = (k * (1.0 + (a - 1.0) * ka_ref[...])).astype(k_out.dtype)
    v = shifted(pa_ref[:, 2 * d:3 * d], ca_ref[7:8, 2 * d:3 * d], mua_ref[:, 2 * d:3 * d])
    if has_vres:
        mix = _sigmoid(v0_ref[...] + _bdot(xc, vup_ref[...]))
        v = v + (vf_ref[...].astype(F32) - v) * mix
    v_out[...] = v.astype(v_out.dtype)

    ca_ref[...] = pa_ref[tt - 8:tt, :].astype(F32)
    cc_ref[...] = pc_ref[tt - 8:tt, :].astype(F32)
    cb_ref[...] = pb_ref[tt - 8:tt, :].astype(F32)


def _unit_lower_inverses(lps, d16):
    n = lps[0].shape[0]
    eye = (lax.broadcasted_iota(jnp.int32, (n, n), 0)
           == lax.broadcasted_iota(jnp.int32, (n, n), 1)).astype(F32)
    ld = [jnp.where(d16, lp, 0.0) for lp in lps]
    e = _each(lambda a, b: a - b, lps, ld)
    l2 = _each(_bdot, ld, ld)
    p = [eye + x for x in ld]
    p = _each(lambda a, b: a + _bdot(a, b), p, l2)
    l4 = _each(_bdot, l2, l2)
    p = _each(lambda a, b: a + _bdot(a, b), p, l4)
    l8 = _each(_bdot, l4, l4)
    dinv = _each(lambda a, b: a + _bdot(a, b), p, l8)
    nm = _each(_bdot, dinv, e)
    n2 = _each(_bdot, nm, nm)
    a1 = _each(lambda a, b: a + _bdot(b, a), dinv, nm)
    return _each(lambda a, b: a + _bdot(b, a), a1, n2)


def _scan_compute(r_ref, k_ref, v_ref, lw_ref, kk_ref, a_ref, g_ref, rk_ref, lnw_ref, lnb_ref,
                  y_ref, s_ref, *, tt, npair):
    c = CHUNK
    nch = tt // c

    @pl.when(pl.program_id(1) == 0)
    def _():
        s_ref[...] = jnp.zeros_like(s_ref)

    row = lax.broadcasted_iota(jnp.int32, (c, LANES), 0)
    lane = lax.broadcasted_iota(jnp.int32, (c, LANES), 1)
    is_a = lane < HEAD_DIM
    s_idx = lane % HEAD_DIM
    strict = s_idx < row
    incl = s_idx <= row
    strict_a = is_a & strict
    strict_b = (~is_a) & strict
    r2 = lax.broadcasted_iota(jnp.int32, (2 * c, LANES), 0)
    l2 = lax.broadcasted_iota(jnp.int32, (2 * c, LANES), 1)
    d16 = (r2 // 16) == (l2 // 16)
    same_head = (r2 // HEAD_DIM) == (l2 // HEAD_DIM)
    tri_incl = (lax.broadcasted_iota(jnp.int32, (c, c), 1)
                <= lax.broadcasted_iota(jnp.int32, (c, c), 0)).astype(BF16)
    gsum = _head_group_matrix(1.0)

    def head_stack(x):
        return jnp.concatenate([jnp.where(is_a, x, 0.0), jnp.where(is_a, 0.0, x)], axis=0)

    def head_stack_swapped(x):
        return jnp.concatenate([jnp.where(is_a, 0.0, x), jnp.where(is_a, x, 0.0)], axis=0)

    elems = [(ci, p) for ci in range(nch) for p in range(npair)]

    def tiles(ref):
        return [ref[ci * c:(ci + 1) * c, p * LANES:(p + 1) * LANES].astype(F32)
                for ci, p in elems]

    kkr = tiles(kk_ref)
    lw = tiles(lw_ref)
    ss = _stacked_dot([x * x for x in kkr], gsum)
    c_in = [_split2_dot(tri_incl, x) for x in lw]
    kk = _each(lambda x, s: x * lax.rsqrt(jnp.maximum(s, 1e-24)), kkr, ss)
    b = _each(lambda x, a: x * a, kk, tiles(a_ref))
    e_tot = [jnp.exp(x[c - 1:c, :]) for x in c_in]
    e_neg = [jnp.exp(-x) for x in c_in]
    at = _each(lambda x, ci_, l: -x * jnp.exp(ci_ - l), kk, c_in, lw)
    rt = _each(lambda x, ci_: x * jnp.exp(ci_), tiles(r_ref), c_in)
    bt = _each(lambda x, en: x * en, b, e_neg)
    kt = _each(lambda x, en: x * en, tiles(k_ref), e_neg)
    v = tiles(v_ref)
    ar = _each(lambda x, y: jnp.concatenate([x, y], axis=0).astype(BF16), at, rt)
    bkt = _each(lambda x, y: jnp.concatenate([x, y], axis=0).astype(BF16), bt, kt)
    lhs = [jnp.concatenate([jnp.where(l2 < HEAD_DIM, x, jnp.zeros_like(x)),
                            jnp.where(l2 < HEAD_DIM, jnp.zeros_like(x), x)], axis=0) for x in ar]
    g_ab = _each(lambda x, y: lax.dot_general(x, y, NT_DIMS, preferred_element_type=F32),
                 lhs, bkt)
    g_a = [x[0:2 * c] for x in g_ab]
    g_b = [pltpu.roll(x[2 * c:4 * c], HEAD_DIM, axis=1) for x in g_ab]
    lps = _each(lambda ga, gb: jnp.concatenate([jnp.where(strict_a, ga[0:c], 0.0),
                                                jnp.where(strict_b, gb[0:c], 0.0)], axis=0),
                g_a, g_b)
    lmk = _each(lambda ga, gb: jnp.concatenate(
        [jnp.where(strict, jnp.where(is_a, gb[0:c], ga[0:c]), 0.0),
         jnp.where(incl, jnp.where(is_a, gb[c:2 * c], ga[c:2 * c]), 0.0)], axis=0), g_a, g_b)
    mrb = _each(lambda ga, gb: jnp.where(incl, jnp.where(is_a, ga[c:2 * c], gb[c:2 * c]),
                                         0.0).astype(BF16), g_a, g_b)
    lv = _each(lambda m, x: _bdot(m, head_stack_swapped(x)), lmk, v)
    bk_h = _each(lambda x, y, et: jnp.concatenate([x * et, y * et], axis=0).astype(BF16),
                 bt, kt, e_tot)
    tps = [x.astype(BF16) for x in _unit_lower_inverses(lps, d16)]

    s = [s_ref[p] for p in range(npair)]
    ys = []
    for ci in range(nch):
        idx = [ci * npair + p for p in range(npair)]
        ah = [lax.dot_general(ar[i], s[p].astype(BF16), NT_DIMS, preferred_element_type=F32)
              for p, i in enumerate(idx)]
        x = [ah[p][0:c] + lv[i][0:c] for p, i in enumerate(idx)]
        tu = [jnp.dot(tps[i], head_stack(x[p]).astype(BF16), preferred_element_type=F32)
              for p, i in enumerate(idx)]
        u = [t[0:c] + t[c:2 * c] for t in tu]
        ys.append([ah[p][c:2 * c] + lv[i][c:2 * c]
                   + jnp.dot(mrb[i], head_stack(u[p]).astype(BF16), preferred_element_type=F32)
                   for p, i in enumerate(idx)])
        uv_t = [jnp.concatenate([u[p], v[i]], axis=0).T.astype(BF16)
                for p, i in enumerate(idx)]
        s = [s[p] * e_tot[i]
             + jnp.where(same_head, jnp.dot(uv_t[p], bk_h[i], preferred_element_type=F32), 0.0)
             for p, i in enumerate(idx)]
    for p in range(npair):
        s_ref[p] = s[p]

    gavg = _head_group_matrix(1.0 / HEAD_DIM)
    cols = [slice(p * LANES, (p + 1) * LANES) for p in range(npair)]
    y = [jnp.concatenate([ys[ci][p] for ci in range(nch)], axis=0) for p in range(npair)]
    rk = [r_ref[:, cs].astype(F32) * k_ref[:, cs].astype(F32) * rk_ref[:, cs] for cs in cols]
    mean = _stacked_dot(y, gavg)
    rk_sum = _stacked_dot(rk, gsum)
    yc = _each(lambda a, b: a - b, y, mean)
    var = _stacked_dot([x * x for x in yc], gavg)
    for p, cs in enumerate(cols):
        yn = yc[p] * lax.rsqrt(var[p] + LNX_EPS) * lnw_ref[:, cs] + lnb_ref[:, cs]
        bonus = rk_sum[p] * v_ref[:, cs].astype(F32)
        y_ref[:, cs] = ((yn + bonus) * g_ref[:, cs].astype(F32)).astype(y_ref.dtype)


def _rwkv_mix_kernel(*refs, tt, npair, has_vres):
    n_prep = 16 if has_vres else 13
    prep_in = refs[0:n_prep]
    rk_ref, lnw_ref, lnb_ref = refs[n_prep:n_prep + 3]
    n_out = 1 if has_vres else 2
    outs = refs[n_prep + 3:n_prep + 3 + n_out]
    s_ref, ca_ref, cc_ref, cb_ref = refs[n_prep + 3 + n_out:n_prep + 7 + n_out]
    work = refs[n_prep + 7 + n_out:]
    _prep_compute(prep_in, work, (ca_ref, cc_ref, cb_ref), has_vres=has_vres, tt=tt)
    if not has_vres:
        outs[1][...] = work[2][...].astype(outs[1].dtype)
    _scan_compute(*work, rk_ref, lnw_ref, lnb_ref, outs[0], s_ref, tt=tt, npair=npair)


def _rwkv_mix(proj, vfirst, lp, r_k, lnx_w, lnx_b, *, bsz, seq, tt):
    n = bsz * seq
    nt = seq // tt
    has_vres = vfirst is not None
    d = D_RWKV
    npair = d // LANES
    row = lambda b, t: (b * nt + t, 0)
    const = lambda b, t: (0, 0)
    rowblk = pl.BlockSpec((tt, d), row)
    par = pl.BlockSpec((1, d), const)
    in_specs = [
        pl.BlockSpec((tt, 3 * d), lambda b, t: (b * nt + t, SEG_A // (3 * d))),
        pl.BlockSpec((tt, 256), lambda b, t: (b * nt + t, SEG_C // 256)),
        pl.BlockSpec((tt, 128), lambda b, t: (b * nt + t, SEG_B // 128)),
        pl.BlockSpec((1, 3 * d), const),
        pl.BlockSpec((1, 256), const),
        pl.BlockSpec((1, 128), const),
        par,
        pl.BlockSpec((128, d), const),
        par,
        pl.BlockSpec((128, d), const),
        pl.BlockSpec((256, d), const),
        par,
        par,
    ]
    args = [proj, proj, proj, lp["mu_a"], lp["mu_c"], lp["mu_b"], lp["w0"], lp["wup"], lp["a0"],
            lp["aup"], lp["gup"], lp["k_k"], lp["k_a"]]
    if has_vres:
        in_specs += [rowblk, par, pl.BlockSpec((256, d), const)]
        args += [vfirst, lp["v0"], lp["vup"]]
    in_specs += [par] * 3
    args += [r_k.reshape(1, d), lnx_w.reshape(1, d), lnx_b.reshape(1, d)]
    n_out = 1 if has_vres else 2
    outs = pl.pallas_call(
        functools.partial(_rwkv_mix_kernel, tt=tt, npair=npair, has_vres=has_vres),
        grid=(bsz, nt),
        in_specs=in_specs,
        out_specs=[rowblk] * n_out,
        out_shape=[jax.ShapeDtypeStruct((n, d), BF16)] * n_out,
        scratch_shapes=[pltpu.VMEM((npair, LANES, LANES), F32),
                        pltpu.VMEM((8, 3 * d), F32), pltpu.VMEM((8, 256), F32),
                        pltpu.VMEM((8, 128), F32)] + [pltpu.VMEM((tt, d), F32)] * 7,
        compiler_params=_cparams(("parallel", "arbitrary")),
        name="rwkv_mix",
    )(*args)
    return (outs[0], None) if has_vres else (outs[0], outs[1])


def _sb_kernel(q_ref, k_ref, v_ref, g_ref, o_ref, vab_ref, *, blk, nblk, ppg, scale):
    i = pl.program_id(2)
    lane = lax.broadcasted_iota(jnp.int32, (blk, LANES), 1)
    is_a = lane < HEAD_DIM
    nh = 2 * ppg
    cols = [slice(p * LANES, (p + 1) * LANES) for p in range(ppg)]

    @pl.when(i == 0)
    def _():
        for p in range(ppg):
            for j in range(nblk):
                vb = v_ref[j * blk:(j + 1) * blk, cols[p]]
                vab_ref[p, j, 0:blk, :] = jnp.where(is_a, vb, jnp.zeros_like(vb))
                vab_ref[p, j, blk:2 * blk, :] = jnp.where(is_a, jnp.zeros_like(vb), vb)

    tq = lax.broadcasted_iota(jnp.int32, (blk, blk), 0)
    sk = lax.broadcasted_iota(jnp.int32, (blk, blk), 1)
    causal = sk < tq
    suffix = (tq > sk).astype(BF16)

    q_heads = []
    for p in range(ppg):
        q = q_ref[:, cols[p]].astype(F32) * (scale * LOG2E)
        q_heads += [jnp.where(is_a, q, 0.0).astype(BF16), jnp.where(is_a, 0.0, q).astype(BF16)]

    def block_steps(js, state, diag):
        carries = list(state[0:nh])
        accs = list(state[nh:nh + ppg])
        kbs = [[k_ref[pl.ds(pl.multiple_of(j * blk, blk), blk), cols[p]] for p in range(ppg)]
               for j in js]
        z2 = [[lax.dot_general(q_heads[h], kb[h // 2], NT_DIMS, preferred_element_type=F32)
               for h in range(nh)] for kb in kbs]
        if diag:
            z2[0] = [jnp.where(causal, z, -jnp.inf) for z in z2[0]]
        sp = [[_softplus2(z) for z in zs] for zs in z2]
        spb = [[s.astype(BF16) for s in ss] for ss in sp]
        lsig = [_each(lambda z, s: z - s, zs, ss) for zs, ss in zip(z2, sp)]
        if diag:
            between = [[jnp.dot(s, suffix, preferred_element_type=F32) for s in ss] for ss in spb]
        else:
            bt_all = jnp.dot(jnp.concatenate([s for ss in spb for s in ss], axis=0), suffix,
                             preferred_element_type=F32)
            between = [[bt_all[(b * nh + h) * blk:(b * nh + h + 1) * blk] for h in range(nh)]
                       for b in range(len(js))]
        for b, j in enumerate(js):
            atts = []
            for h in range(nh):
                atts.append(jnp.exp2((lsig[b][h] - between[b][h] - carries[h]).astype(BF16)))
                carries[h] = carries[h] + between[b][h][:, 0:1] + sp[b][h][:, 0:1]
            for p in range(ppg):
                accs[p] = accs[p] + jnp.dot(jnp.concatenate(atts[2 * p:2 * p + 2], axis=1),
                                            vab_ref[p, j], preferred_element_type=F32)
        return tuple(carries) + tuple(accs)

    state = (jnp.zeros((blk, 1), F32),) * nh + (jnp.zeros((blk, LANES), F32),) * ppg
    first = jnp.where(i == 0, 0, 2 - i % 2)
    state = lax.switch(first, [lambda s: block_steps([i], s, True),
                               lambda s: block_steps([i, i - 1], s, True),
                               lambda s: block_steps([i, i - 1, i - 2], s, True)], state)

    def body(m, s):
        j = i - 1 - first - 2 * m
        return block_steps([j, j - 1], s, False)

    state = lax.fori_loop(0, (i - first) // 2, body, state)

    gavg = _head_group_matrix(1.0 / HEAD_DIM)
    accs = [state[nh + p] for p in range(ppg)]
    ms = [_bdot(acc * acc, gavg) for acc in accs]
    for p in range(ppg):
        o_ref[:, cols[p]] = (accs[p] * lax.rsqrt(ms[p] + NORM_EPS)
                             * g_ref[:, cols[p]]).astype(o_ref.dtype)


def _sb_attention(qkv, gain, *, bsz, seq, blk, ppg):
    n = bsz * seq
    nq = seq // blk
    ngrp = D_SB // (LANES * ppg)
    w = LANES * ppg
    return pl.pallas_call(
        functools.partial(_sb_kernel, blk=blk, nblk=nq, ppg=ppg, scale=1.0 / (HEAD_DIM ** 0.5)),
        grid=(bsz, ngrp, nq),
        in_specs=[
            pl.BlockSpec((blk, w), lambda b, h, i: (b * nq + i, h)),
            pl.BlockSpec((seq, w), lambda b, h, i: (b, ngrp + h)),
            pl.BlockSpec((seq, w), lambda b, h, i: (b, 2 * ngrp + h)),
            pl.BlockSpec((1, w), lambda b, h, i: (0, h)),
        ],
        out_specs=pl.BlockSpec((blk, w), lambda b, h, i: (b * nq + i, h)),
        out_shape=jax.ShapeDtypeStruct((n, D_SB), BF16),
        scratch_shapes=[pltpu.VMEM((ppg, nq, 2 * blk, LANES), BF16)],
        compiler_params=_cparams(("parallel", "parallel", "arbitrary")),
        name="sb_attention",
    )(qkv, qkv, qkv, gain.reshape(1, D_SB))


def _layer_params(l, d_model, w_in, w_in_vres, mu, mu_vres, w0, w_up, a0, a_up, g_up, v0, v_up,
                  k_k, k_a):
    d = D_RWKV
    wl = w_in[l]
    c_dw = 3 * d
    c_da = c_dw + W_LORA
    c_dg = c_da + A_LORA
    c_sb = c_dg + G_LORA
    zcol = lambda m: jnp.zeros((d_model, m), F32)
    zrow = lambda m: jnp.zeros((m,), F32)
    if l > 0:
        vres_w, vres_mu = w_in_vres[l - 1], mu_vres[l - 1]
    else:
        vres_w, vres_mu = zcol(V_LORA), zrow(V_LORA)
    w_rwkv = jnp.concatenate([
        wl[:, 0:c_dw],
        wl[:, c_dg:c_sb], vres_w, zcol(256 - G_LORA - V_LORA),
        wl[:, c_dw:c_dg],
        zcol(P_RWKV - SEG_B - 128),
    ], axis=1).astype(BF16)
    mul = mu[l]
    lp = {
        "w_rwkv": w_rwkv,
        "w_sb": wl[:, c_sb:].astype(BF16),
        "mu_a": mul[0:c_dw].reshape(1, -1),
        "mu_c": jnp.concatenate([mul[c_dg:c_sb], vres_mu,
                                 zrow(256 - G_LORA - V_LORA)]).reshape(1, -1),
        "mu_b": mul[c_dw:c_dg].reshape(1, -1),
        "w0": w0[l].reshape(1, d),
        "a0": a0[l].reshape(1, d),
        "k_k": k_k[l].reshape(1, d),
        "k_a": k_a[l].reshape(1, d),
        "wup": jnp.concatenate([w_up[l], jnp.zeros((A_LORA, d), F32)], axis=0).astype(BF16),
        "aup": jnp.concatenate([jnp.zeros((W_LORA, d), F32), a_up[l]], axis=0).astype(BF16),
        "gup": jnp.concatenate([g_up[l], jnp.zeros((256 - G_LORA, d), F32)],
                               axis=0).astype(BF16),
    }
    if l > 0:
        lp["v0"] = v0[l - 1].reshape(1, d)
        lp["vup"] = jnp.concatenate([jnp.zeros((G_LORA, d), F32), v_up[l - 1],
                                     jnp.zeros((256 - G_LORA - V_LORA, d), F32)],
                                    axis=0).astype(BF16)
    return lp


def kernel(x, pre_mix_g, post_mix_g, pre_mlp_g, post_mlp_g, w_in, w_in_vres, mu, mu_vres, w0, w_up,
           a0, a_up, g_up, v0, v_up, k_k, k_a, r_k, lnx_w, lnx_b, sb_out_g, w_out, w_ff_up,
           w_ff_down):
    bsz, seq, d_model = x.shape
    depth = w_in.shape[0]
    n = bsz * seq
    assert w_in.shape[2] == 3 * D_RWKV + W_LORA + A_LORA + G_LORA + 3 * D_SB
    tm_mm = min(1024, n)
    tm_res = min(512, n)
    xf = x.reshape(n, d_model)
    h = _norm_cast(xf, pre_mix_g[0], tm=tm_res)
    vfirst = None
    for l in range(depth):
        lp = _layer_params(l, d_model, w_in, w_in_vres, mu, mu_vres, w0, w_up, a0, a_up, g_up,
                           v0, v_up, k_k, k_a)
        proj = _matmul(h, lp["w_rwkv"], relu2=False, tm=tm_mm, tn=P_RWKV // 2, name="proj_rwkv")
        qkv = _matmul(h, lp["w_sb"], relu2=False, tm=tm_mm, tn=1536, name="proj_sb")
        y_r, v_l = _rwkv_mix(proj, vfirst, lp, r_k[l], lnx_w[l], lnx_b[l],
                             bsz=bsz, seq=seq, tt=min(128, seq))
        if l == 0:
            vfirst = v_l
        y_s = _sb_attention(qkv, sb_out_g[l], bsz=bsz, seq=seq, blk=min(256, seq), ppg=2)
        xf, h = _mix_out(y_r, y_s, w_out[l].astype(BF16), xf, post_mix_g[l], pre_mlp_g[l],
                         tm=tm_res)
        ff = _matmul(h, w_ff_up[l].astype(BF16), relu2=True, tm=tm_mm, tn=2048, name="ffn_up")
        g_next = pre_mix_g[l + 1] if l + 1 < depth else None
        xf, h = _ffn_down(ff, w_ff_down[l].astype(BF16), xf, post_mlp_g[l], g_next,
                          tm=tm_res, tk=2048)
    return xf.reshape(bsz, seq, d_model)
```

```python
import functools

import jax
import jax.numpy as jnp
from jax import lax
from jax.experimental import pallas as pl
from jax.experimental.pallas import tpu as pltpu

F32 = jnp.float32
BF16 = jnp.bfloat16

HEAD_DIM = 64
LANES = 128
NORM_EPS = 1e-6
LNX_EPS = 64e-5
CHUNK = 64
VMEM_LIMIT = 48 * 1024 * 1024
LOG2E = 1.4426950408889634
DECAY_SCALE = 0.6065306597126334

D_RWKV = 1024
D_SB = 1024
W_LORA, A_LORA, V_LORA, G_LORA = 64, 64, 32, 160
SEG_A = 0
SEG_C = 3 * D_RWKV
SEG_B = SEG_C + 256
P_RWKV = 3584

NT_DIMS = (((1,), (1,)), ((), ()))


def _cparams(sem):
    return pltpu.CompilerParams(dimension_semantics=sem, vmem_limit_bytes=VMEM_LIMIT)


def _bdot(a, b):
    return jnp.dot(a.astype(BF16), b.astype(BF16), preferred_element_type=F32)


def _split2_dot(a_exact, x):
    hi = x.astype(BF16)
    lo = (x - hi.astype(F32)).astype(BF16)
    return (jnp.dot(a_exact, hi, preferred_element_type=F32)
            + jnp.dot(a_exact, lo, preferred_element_type=F32))


def _head_group_matrix(scale):
    i = lax.broadcasted_iota(jnp.int32, (LANES, LANES), 0) // HEAD_DIM
    j = lax.broadcasted_iota(jnp.int32, (LANES, LANES), 1) // HEAD_DIM
    return jnp.where(i == j, scale, 0.0).astype(BF16)


def _sigmoid(x):
    return 0.5 * jnp.tanh(0.5 * x) + 0.5


def _neg_abs(x):
    bits = lax.bitcast_convert_type(x, jnp.uint32) | jnp.uint32(0x80000000)
    return lax.bitcast_convert_type(bits, F32)


def _softplus2(z2):
    return jnp.maximum(z2, 0.0) + jnp.log2(1.0 + jnp.exp2(_neg_abs(z2)))


def _stacked_dot(xs, w):
    rows = xs[0].shape[0]
    out = _bdot(jnp.concatenate(xs, axis=0), w)
    return [out[i * rows:(i + 1) * rows] for i in range(len(xs))]


def _each(fn, *lists):
    return [fn(*xs) for xs in zip(*lists)]


def _rms_scaled(x, g):
    ms = jnp.mean(x * x, axis=-1, keepdims=True)
    return x * lax.rsqrt(ms + NORM_EPS) * g


def _norm_cast_kernel(x_ref, g_ref, h_ref):
    h_ref[...] = _rms_scaled(x_ref[...], g_ref[...]).astype(h_ref.dtype)


def _norm_cast(x, g, *, tm):
    n, d = x.shape
    return pl.pallas_call(
        _norm_cast_kernel,
        grid=(n // tm,),
        in_specs=[pl.BlockSpec((tm, d), lambda i: (i, 0)), pl.BlockSpec((1, d), lambda i: (0, 0))],
        out_specs=pl.BlockSpec((tm, d), lambda i: (i, 0)),
        out_shape=jax.ShapeDtypeStruct((n, d), BF16),
        compiler_params=_cparams(("parallel",)),
        name="norm_cast",
    )(x, g.reshape(1, d))


def _matmul_kernel(a_ref, w_ref, o_ref, *, relu2):
    y = jnp.dot(a_ref[...], w_ref[...], preferred_element_type=F32)
    if relu2:
        y = jnp.square(jnp.maximum(y, 0.0))
    o_ref[...] = y.astype(o_ref.dtype)


def _matmul(a, w, *, relu2, tm, tn, name):
    n, kdim = a.shape
    p = w.shape[1]
    return pl.pallas_call(
        functools.partial(_matmul_kernel, relu2=relu2),
        grid=(n // tm, p // tn),
        in_specs=[
            pl.BlockSpec((tm, kdim), lambda i, j: (i, 0)),
            pl.BlockSpec((kdim, tn), lambda i, j: (0, j)),
        ],
        out_specs=pl.BlockSpec((tm, tn), lambda i, j: (i, j)),
        out_shape=jax.ShapeDtypeStruct((n, p), BF16),
        compiler_params=_cparams(("parallel", "arbitrary")),
        name=name,
    )(a, w)


def _residual_epilogue(acc, x_ref, g_ref, gn_ref, o_ref, h_ref):
    xn = x_ref[...] + _rms_scaled(acc, g_ref[...])
    o_ref[...] = xn
    if h_ref is not None:
        h_ref[...] = _rms_scaled(xn, gn_ref[...]).astype(h_ref.dtype)


def _mix_out_kernel(a1_ref, a2_ref, w1_ref, w2_ref, x_ref, g_ref, gn_ref, o_ref, h_ref):
    acc = jnp.dot(a1_ref[...], w1_ref[...], preferred_element_type=F32)
    acc = acc + jnp.dot(a2_ref[...], w2_ref[...], preferred_element_type=F32)
    _residual_epilogue(acc, x_ref, g_ref, gn_ref, o_ref, h_ref)


def _mix_out(a1, a2, w, x, g, g_next, *, tm):
    n, d = x.shape
    k1 = a1.shape[1]
    k2 = a2.shape[1]
    assert k1 == k2 and w.shape[0] == k1 + k2
    return pl.pallas_call(
        _mix_out_kernel,
        grid=(n // tm,),
        in_specs=[
            pl.BlockSpec((tm, k1), lambda i: (i, 0)),
            pl.BlockSpec((tm, k2), lambda i: (i, 0)),
            pl.BlockSpec((k1, d), lambda i: (0, 0)),
            pl.BlockSpec((k2, d), lambda i: (1, 0)),
            pl.BlockSpec((tm, d), lambda i: (i, 0)),
            pl.BlockSpec((1, d), lambda i: (0, 0)),
            pl.BlockSpec((1, d), lambda i: (0, 0)),
        ],
        out_specs=[pl.BlockSpec((tm, d), lambda i: (i, 0))] * 2,
        out_shape=[jax.ShapeDtypeStruct((n, d), F32), jax.ShapeDtypeStruct((n, d), BF16)],
        compiler_params=_cparams(("parallel",)),
        name="mix_out",
    )(a1, a2, w, w, x, g.reshape(1, d), g_next.reshape(1, d))


def _ffn_down_kernel(*refs, nk, emit_h):
    if emit_h:
        a_ref, w_ref, x_ref, g_ref, gn_ref, o_ref, h_ref = refs
    else:
        a_ref, w_ref, x_ref, g_ref, o_ref = refs
        gn_ref = h_ref = None
    k = pl.program_id(1)
    part = jnp.dot(a_ref[...], w_ref[...], preferred_element_type=F32)

    @pl.when(k == 0)
    def _():
        o_ref[...] = part

    @pl.when(k > 0)
    def _():
        o_ref[...] += part

    @pl.when(k == nk - 1)
    def _():
        _residual_epilogue(o_ref[...], x_ref, g_ref, gn_ref, o_ref, h_ref)


def _ffn_down(a, w, x, g, g_next, *, tm, tk):
    n, d = x.shape
    kdim = a.shape[1]
    nk = kdim // tk
    emit_h = g_next is not None
    row = pl.BlockSpec((tm, d), lambda i, k: (i, 0))
    gain = pl.BlockSpec((1, d), lambda i, k: (0, 0))
    in_specs = [pl.BlockSpec((tm, tk), lambda i, k: (i, k)),
                pl.BlockSpec((tk, d), lambda i, k: (k, 0)), row, gain]
    args = [a, w, x, g.reshape(1, d)]
    out_specs = [row]
    out_shape = [jax.ShapeDtypeStruct((n, d), F32)]
    if emit_h:
        in_specs.append(gain)
        args.append(g_next.reshape(1, d))
        out_specs.append(row)
        out_shape.append(jax.ShapeDtypeStruct((n, d), BF16))
    outs = pl.pallas_call(
        functools.partial(_ffn_down_kernel, nk=nk, emit_h=emit_h),
        grid=(n // tm, nk),
        in_specs=in_specs,
        out_specs=out_specs,
        out_shape=out_shape,
        compiler_params=_cparams(("parallel", "arbitrary")),
        name="ffn_down",
    )(*args)
    return (outs[0], outs[1]) if emit_h else (outs[0], None)


def _prep_compute(in_refs, out_refs, carry_refs, *, has_vres, tt):
    (pa_ref, pc_ref, pb_ref, mua_ref, muc_ref, mub_ref, w0_ref, wup_ref, a0_ref, aup_ref,
     gup_ref, kk_ref, ka_ref) = in_refs[0:13]
    if has_vres:
        vf_ref, v0_ref, vup_ref = in_refs[13:16]
    r_out, k_out, v_out, lw_out, kk_out, a_out, g_out = out_refs
    ca_ref, cc_ref, cb_ref = carry_refs

    @pl.when(pl.program_id(1) == 0)
    def _():
        ca_ref[...] = jnp.zeros_like(ca_ref)
        cc_ref[...] = jnp.zeros_like(cc_ref)
        cb_ref[...] = jnp.zeros_like(cb_ref)

    down = (lax.broadcasted_iota(jnp.int32, (tt, tt), 1) + 1
            == lax.broadcasted_iota(jnp.int32, (tt, tt), 0)).astype(BF16)

    def shifted(pb16, prev_row, mu):
        p = pb16.astype(F32)
        prev = jnp.dot(down, pb16, preferred_element_type=F32)
        row = lax.broadcasted_iota(jnp.int32, p.shape, 0)
        prev = jnp.where(row == 0, prev_row, prev)
        return p + (prev - p) * mu

    xb = shifted(pb_ref[...], cb_ref[7:8, :], mub_ref[...])
    xc = shifted(pc_ref[...], cc_ref[7:8, :], muc_ref[...])

    wl = w0_ref[...] + _bdot(jnp.tanh(xb), wup_ref[...])
    lw_out[...] = -DECAY_SCALE * _sigmoid(wl)
    a = _sigmoid(a0_ref[...] + _bdot(xb, aup_ref[...]))
    a_out[...] = a.astype(a_out.dtype)
    g_out[...] = _bdot(_sigmoid(xc), gup_ref[...]).astype(g_out.dtype)

    d = D_RWKV
    r_out[...] = shifted(pa_ref[:, 0:d], ca_ref[7:8, 0:d], mua_ref[:, 0:d]).astype(r_out.dtype)
    k = shifted(pa_ref[:, d:2 * d], ca_ref[7:8, d:2 * d], mua_ref[:, d:2 * d])
    kk_out[...] = (k * kk_ref[...]).astype(kk_out.dtype)
    k_out[...] = (k * (1.0 + (a - 1.0) * ka_ref[...])).astype(k_out.dtype)
    v = shifted(pa_ref[:, 2 * d:3 * d], ca_ref[7:8, 2 * d:3 * d], mua_ref[:, 2 * d:3 * d])
    if has_vres:
        mix = _sigmoid(v0_ref[...] + _bdot(xc, vup_ref[...]))
        v = v + (vf_ref[...].astype(F32) - v) * mix
    v_out[...] = v.astype(v_out.dtype)

    ca_ref[...] = pa_ref[tt - 8:tt, :].astype(F32)
    cc_ref[...] = pc_ref[tt - 8:tt, :].astype(F32)
    cb_ref[...] = pb_ref[tt - 8:tt, :].astype(F32)


def _unit_lower_inverses(lps, d16):
    n = lps[0].shape[0]
    eye = (lax.broadcasted_iota(jnp.int32, (n, n), 0)
           == lax.broadcasted_iota(jnp.int32, (n, n), 1)).astype(F32)
    ld = [jnp.where(d16, lp, 0.0) for lp in lps]
    e = _each(lambda a, b: a - b, lps, ld)
    l2 = _each(_bdot, ld, ld)
    p = [eye + x for x in ld]
    p = _each(lambda a, b: a + _bdot(a, b), p, l2)
    l4 = _each(_bdot, l2, l2)
    p = _each(lambda a, b: a + _bdot(a, b), p, l4)
    l8 = _each(_bdot, l4, l4)
    dinv = _each(lambda a, b: a + _bdot(a, b), p, l8)
    nm = _each(_bdot, dinv, e)
    n2 = _each(_bdot, nm, nm)
    a1 = _each(lambda a, b: a + _bdot(b, a), dinv, nm)
    return _each(lambda a, b: a + _bdot(b, a), a1, n2)


def _scan_compute(r_ref, k_ref, v_ref, lw_ref, kk_ref, a_ref, g_ref, rk_ref, lnw_ref, lnb_ref,
                  y_ref, s_ref, *, tt, npair):
    c = CHUNK
    nch = tt // c

    @pl.when(pl.program_id(1) == 0)
    def _():
        s_ref[...] = jnp.zeros_like(s_ref)

    row = lax.broadcasted_iota(jnp.int32, (c, LANES), 0)
    lane = lax.broadcasted_iota(jnp.int32, (c, LANES), 1)
    is_a = lane < HEAD_DIM
    s_idx = lane % HEAD_DIM
    strict = s_idx < row
    incl = s_idx <= row
    strict_a = is_a & strict
    strict_b = (~is_a) & strict
    r2 = lax.broadcasted_iota(jnp.int32, (2 * c, LANES), 0)
    l2 = lax.broadcasted_iota(jnp.int32, (2 * c, LANES), 1)
    d16 = (r2 // 16) == (l2 // 16)
    same_head = (r2 // HEAD_DIM) == (l2 // HEAD_DIM)
    tri_incl = (lax.broadcasted_iota(jnp.int32, (c, c), 1)
                <= lax.broadcasted_iota(jnp.int32, (c, c), 0)).astype(BF16)
    gsum = _head_group_matrix(1.0)

    def head_stack(x):
        return jnp.concatenate([jnp.where(is_a, x, 0.0), jnp.where(is_a, 0.0, x)], axis=0)

    def head_stack_swapped(x):
        return jnp.concatenate([jnp.where(is_a, 0.0, x), jnp.where(is_a, x, 0.0)], axis=0)

    elems = [(ci, p) for ci in range(nch) for p in range(npair)]

    def tiles(ref):
        return [ref[ci * c:(ci + 1) * c, p * LANES:(p + 1) * LANES].astype(F32)
                for ci, p in elems]

    kkr = tiles(kk_ref)
    lw = tiles(lw_ref)
    ss = _stacked_dot([x * x for x in kkr], gsum)
    c_in = [_split2_dot(tri_incl, x) for x in lw]
    kk = _each(lambda x, s: x * lax.rsqrt(jnp.maximum(s, 1e-24)), kkr, ss)
    b = _each(lambda x, a: x * a, kk, tiles(a_ref))
    e_tot = [jnp.exp(x[c - 1:c, :]) for x in c_in]
    e_neg = [jnp.exp(-x) for x in c_in]
    at = _each(lambda x, ci_, l: -x * jnp.exp(ci_ - l), kk, c_in, lw)
    rt = _each(lambda x, ci_: x * jnp.exp(ci_), tiles(r_ref), c_in)
    bt = _each(lambda x, en: x * en, b, e_neg)
    kt = _each(lambda x, en: x * en, tiles(k_ref), e_neg)
    v = tiles(v_ref)
    ar = _each(lambda x, y: jnp.concatenate([x, y], axis=0).astype(BF16), at, rt)
    bkt = _each(lambda x, y: jnp.concatenate([x, y], axis=0).astype(BF16), bt, kt)
    lhs = [jnp.concatenate([jnp.where(l2 < HEAD_DIM, x, jnp.zeros_like(x)),
                            jnp.where(l2 < HEAD_DIM, jnp.zeros_like(x), x)], axis=0) for x in ar]
    g_ab = _each(lambda x, y: lax.dot_general(x, y, NT_DIMS, preferred_element_type=F32),
                 lhs, bkt)
    g_a = [x[0:2 * c] for x in g_ab]
    g_b = [pltpu.roll(x[2 * c:4 * c], HEAD_DIM, axis=1) for x in g_ab]
    lps = _each(lambda ga, gb: jnp.concatenate([jnp.where(strict_a, ga[0:c], 0.0),
                                                jnp.where(strict_b, gb[0:c], 0.0)], axis=0),
                g_a, g_b)
    lmk = _each(lambda ga, gb: jnp.concatenate(
        [jnp.where(strict, jnp.where(is_a, gb[0:c], ga[0:c]), 0.0),
         jnp.where(incl, jnp.where(is_a, gb[c:2 * c], ga[c:2 * c]), 0.0)], axis=0), g_a, g_b)
    mrb = _each(lambda ga, gb: jnp.where(incl, jnp.where(is_a, ga[c:2 * c], gb[c:2 * c]),
                                         0.0).astype(BF16), g_a, g_b)
    lv = _each(lambda m, x: _bdot(m, head_stack_swapped(x)), lmk, v)
    bk_h = _each(lambda x, y, et: jnp.concatenate([x * et, y * et], axis=0).astype(BF16),
                 bt, kt, e_tot)
    tps = [x.astype(BF16) for x in _unit_lower_inverses(lps, d16)]

    s = [s_ref[p] for p in range(npair)]
    ys = []
    for ci in range(nch):
        idx = [ci * npair + p for p in range(npair)]
        ah = [lax.dot_general(ar[i], s[p].astype(BF16), NT_DIMS, preferred_element_type=F32)
              for p, i in enumerate(idx)]
        x = [ah[p][0:c] + lv[i][0:c] for p, i in enumerate(idx)]
        tu = [jnp.dot(tps[i], head_stack(x[p]).astype(BF16), preferred_element_type=F32)
              for p, i in enumerate(idx)]
        u = [t[0:c] + t[c:2 * c] for t in tu]
        ys.append([ah[p][c:2 * c] + lv[i][c:2 * c]
                   + jnp.dot(mrb[i], head_stack(u[p]).astype(BF16), preferred_element_type=F32)
                   for p, i in enumerate(idx)])
        uv_t = [jnp.concatenate([u[p], v[i]], axis=0).T.astype(BF16)
                for p, i in enumerate(idx)]
        s = [s[p] * e_tot[i]
             + jnp.where(same_head, jnp.dot(uv_t[p], bk_h[i], preferred_element_type=F32), 0.0)
             for p, i in enumerate(idx)]
    for p in range(npair):
        s_ref[p] = s[p]

    gavg = _head_group_matrix(1.0 / HEAD_DIM)
    cols = [slice(p * LANES, (p + 1) * LANES) for p in range(npair)]
    y = [jnp.concatenate([ys[ci][p] for ci in range(nch)], axis=0) for p in range(npair)]
    rk = [r_ref[:, cs].astype(F32) * k_ref[:, cs].astype(F32) * rk_ref[:, cs] for cs in cols]
    mean = _stacked_dot(y, gavg)
    rk_sum = _stacked_dot(rk, gsum)
    yc = _each(lambda a, b: a - b, y, mean)
    var = _stacked_dot([x * x for x in yc], gavg)
    for p, cs in enumerate(cols):
        yn = yc[p] * lax.rsqrt(var[p] + LNX_EPS) * lnw_ref[:, cs] + lnb_ref[:, cs]
        bonus = rk_sum[p] * v_ref[:, cs].astype(F32)
        y_ref[:, cs] = ((yn + bonus) * g_ref[:, cs].astype(F32)).astype(y_ref.dtype)


def _rwkv_mix_kernel(*refs, tt, npair, has_vres):
    n_prep = 16 if has_vres else 13
    prep_in = refs[0:n_prep]
    rk_ref, lnw_ref, lnb_ref = refs[n_prep:n_prep + 3]
    n_out = 1 if has_vres else 2
    outs = refs[n_prep + 3:n_prep + 3 + n_out]
    s_ref, ca_ref, cc_ref, cb_ref = refs[n_prep + 3 + n_out:n_prep + 7 + n_out]
    work = refs[n_prep + 7 + n_out:]
    _prep_compute(prep_in, work, (ca_ref, cc_ref, cb_ref), has_vres=has_vres, tt=tt)
    if not has_vres:
        outs[1][...] = work[2][...].astype(outs[1].dtype)
    _scan_compute(*work, rk_ref, lnw_ref, lnb_ref, outs[0], s_ref, tt=tt, npair=npair)


def _rwkv_mix(proj, vfirst, lp, r_k, lnx_w, lnx_b, *, bsz, seq, tt):
    n = bsz * seq
    nt = seq // tt
    has_vres = vfirst is not None
    d = D_RWKV
    npair = d // LANES
    row = lambda b, t: (b * nt + t, 0)
    const = lambda b, t: (0, 0)
    rowblk = pl.BlockSpec((tt, d), row)
    par = pl.BlockSpec((1, d), const)
    in_specs = [
        pl.BlockSpec((tt, 3 * d), lambda b, t: (b * nt + t, SEG_A // (3 * d))),
        pl.BlockSpec((tt, 256), lambda b, t: (b * nt + t, SEG_C // 256)),
        pl.BlockSpec((tt, 128), lambda b, t: (b * nt + t, SEG_B // 128)),
        pl.BlockSpec((1, 3 * d), const),
        pl.BlockSpec((1, 256), const),
        pl.BlockSpec((1, 128), const),
        par,
        pl.BlockSpec((128, d), const),
        par,
        pl.BlockSpec((128, d), const),
        pl.BlockSpec((256, d), const),
        par,
        par,
    ]
    args = [proj, proj, proj, lp["mu_a"], lp["mu_c"], lp["mu_b"], lp["w0"], lp["wup"], lp["a0"],
            lp["aup"], lp["gup"], lp["k_k"], lp["k_a"]]
    if has_vres:
        in_specs += [rowblk, par, pl.BlockSpec((256, d), const)]
        args += [vfirst, lp["v0"], lp["vup"]]
    in_specs += [par] * 3
    args += [r_k.reshape(1, d), lnx_w.reshape(1, d), lnx_b.reshape(1, d)]
    n_out = 1 if has_vres else 2
    outs = pl.pallas_call(
        functools.partial(_rwkv_mix_kernel, tt=tt, npair=npair, has_vres=has_vres),
        grid=(bsz, nt),
        in_specs=in_specs,
        out_specs=[rowblk] * n_out,
        out_shape=[jax.ShapeDtypeStruct((n, d), BF16)] * n_out,
        scratch_shapes=[pltpu.VMEM((npair, LANES, LANES), F32),
                        pltpu.VMEM((8, 3 * d), F32), pltpu.VMEM((8, 256), F32),
                        pltpu.VMEM((8, 128), F32)] + [pltpu.VMEM((tt, d), F32)] * 7,
        compiler_params=_cparams(("parallel", "arbitrary")),
        name="rwkv_mix",
    )(*args)
    return (outs[0], None) if has_vres else (outs[0], outs[1])


def _sb_kernel(q_ref, k_ref, v_ref, g_ref, o_ref, vab_ref, *, blk, nblk, ppg, scale):
    i = pl.program_id(2)
    lane = lax.broadcasted_iota(jnp.int32, (blk, LANES), 1)
    is_a = lane < HEAD_DIM
    nh = 2 * ppg
    cols = [slice(p * LANES, (p + 1) * LANES) for p in range(ppg)]

    @pl.when(i == 0)
    def _():
        for p in range(ppg):
            for j in range(nblk):
                vb = v_ref[j * blk:(j + 1) * blk, cols[p]]
                vab_ref[p, j, 0:blk, :] = jnp.where(is_a, vb, jnp.zeros_like(vb))
                vab_ref[p, j, blk:2 * blk, :] = jnp.where(is_a, jnp.zeros_like(vb), vb)

    tq = lax.broadcasted_iota(jnp.int32, (blk, blk), 0)
    sk = lax.broadcasted_iota(jnp.int32, (blk, blk), 1)
    causal = sk < tq
    suffix = (tq > sk).astype(BF16)

    q_heads = []
    for p in range(ppg):
        q = q_ref[:, cols[p]].astype(F32) * (scale * LOG2E)
        q_heads += [jnp.where(is_a, q, 0.0).astype(BF16), jnp.where(is_a, 0.0, q).astype(BF16)]

    def block_steps(js, state, diag):
        carries = list(state[0:nh])
        accs = list(state[nh:nh + ppg])
        kbs = [[k_ref[pl.ds(pl.multiple_of(j * blk, blk), blk), cols[p]] for p in range(ppg)]
               for j in js]
        z2 = [[lax.dot_general(q_heads[h], kb[h // 2], NT_DIMS, preferred_element_type=F32)
               for h in range(nh)] for kb in kbs]
        if diag:
            z2[0] = [jnp.where(causal, z, -jnp.inf) for z in z2[0]]
        sp = [[_softplus2(z) for z in zs] for zs in z2]
        spb = [[s.astype(BF16) for s in ss] for ss in sp]
        lsig = [_each(lambda z, s: z - s, zs, ss) for zs, ss in zip(z2, sp)]
        if diag:
            between = [[jnp.dot(s, suffix, preferred_element_type=F32) for s in ss] for ss in spb]
        else:
            bt_all = jnp.dot(jnp.concatenate([s for ss in spb for s in ss], axis=0), suffix,
                             preferred_element_type=F32)
            between = [[bt_all[(b * nh + h) * blk:(b * nh + h + 1) * blk] for h in range(nh)]
                       for b in range(len(js))]
        for b, j in enumerate(js):
            atts = []
            for h in range(nh):
                atts.append(jnp.exp2((lsig[b][h] - between[b][h] - carries[h]).astype(BF16)))
                carries[h] = carries[h] + between[b][h][:, 0:1] + sp[b][h][:, 0:1]
            for p in range(ppg):
                accs[p] = accs[p] + jnp.dot(jnp.concatenate(atts[2 * p:2 * p + 2], axis=1),
                                            vab_ref[p, j], preferred_element_type=F32)
        return tuple(carries) + tuple(accs)

    state = (jnp.zeros((blk, 1), F32),) * nh + (jnp.zeros((blk, LANES), F32),) * ppg
    first = jnp.where(i == 0, 0, 2 - i % 2)
    state = lax.switch(first, [lambda s: block_steps([i], s, True),
                               lambda s: block_steps([i, i - 1], s, True),
                               lambda s: block_steps([i, i - 1, i - 2], s, True)], state)

    def body(m, s):
        j = i - 1 - first - 2 * m
        return block_steps([j, j - 1], s, False)

    state = lax.fori_loop(0, (i - first) // 2, body, state)

    gavg = _head_group_matrix(1.0 / HEAD_DIM)
    accs = [state[nh + p] for p in range(ppg)]
    ms = [_bdot(acc * acc, gavg) for acc in accs]
    for p in range(ppg):
        o_ref[:, cols[p]] = (accs[p] * lax.rsqrt(ms[p] + NORM_EPS)
                             * g_ref[:, cols[p]]).astype(o_ref.dtype)


def _sb_attention(qkv, gain, *, bsz, seq, blk, ppg):
    n = bsz * seq
    nq = seq // blk
    ngrp = D_SB // (LANES * ppg)
    w = LANES * ppg
    return pl.pallas_call(
        functools.partial(_sb_kernel, blk=blk, nblk=nq, ppg=ppg, scale=1.0 / (HEAD_DIM ** 0.5)),
        grid=(bsz, ngrp, nq),
        in_specs=[
            pl.BlockSpec((blk, w), lambda b, h, i: (b * nq + i, h)),
            pl.BlockSpec((seq, w), lambda b, h, i: (b, ngrp + h)),
            pl.BlockSpec((seq, w), lambda b, h, i: (b, 2 * ngrp + h)),
            pl.BlockSpec((1, w), lambda b, h, i: (0, h)),
        ],
        out_specs=pl.BlockSpec((blk, w), lambda b, h, i: (b * nq + i, h)),
        out_shape=jax.ShapeDtypeStruct((n, D_SB), BF16),
        scratch_shapes=[pltpu.VMEM((ppg, nq, 2 * blk, LANES), BF16)],
        compiler_params=_cparams(("parallel", "parallel", "arbitrary")),
        name="sb_attention",
    )(qkv, qkv, qkv, gain.reshape(1, D_SB))


def _layer_params(l, d_model, w_in, w_in_vres, mu, mu_vres, w0, w_up, a0, a_up, g_up, v0, v_up,
                  k_k, k_a):
    d = D_RWKV
    wl = w_in[l]
    c_dw = 3 * d
    c_da = c_dw + W_LORA
    c_dg = c_da + A_LORA
    c_sb = c_dg + G_LORA
    zcol = lambda m: jnp.zeros((d_model, m), F32)
    zrow = lambda m: jnp.zeros((m,), F32)
    if l > 0:
        vres_w, vres_mu = w_in_vres[l - 1], mu_vres[l - 1]
    else:
        vres_w, vres_mu = zcol(V_LORA), zrow(V_LORA)
    w_rwkv = jnp.concatenate([
        wl[:, 0:c_dw],
        wl[:, c_dg:c_sb], vres_w, zcol(256 - G_LORA - V_LORA),
        wl[:, c_dw:c_dg],
        zcol(P_RWKV - SEG_B - 128),
    ], axis=1).astype(BF16)
    mul = mu[l]
    lp = {
        "w_rwkv": w_rwkv,
        "w_sb": wl[:, c_sb:].astype(BF16),
        "mu_a": mul[0:c_dw].reshape(1, -1),
        "mu_c": jnp.concatenate([mul[c_dg:c_sb], vres_mu,
                                 zrow(256 - G_LORA - V_LORA)]).reshape(1, -1),
        "mu_b": mul[c_dw:c_dg].reshape(1, -1),
        "w0": w0[l].reshape(1, d),
        "a0": a0[l].reshape(1, d),
        "k_k": k_k[l].reshape(1, d),
        "k_a": k_a[l].reshape(1, d),
        "wup": jnp.concatenate([w_up[l], jnp.zeros((A_LORA, d), F32)], axis=0).astype(BF16),
        "aup": jnp.concatenate([jnp.zeros((W_LORA, d), F32), a_up[l]], axis=0).astype(BF16),
        "gup": jnp.concatenate([g_up[l], jnp.zeros((256 - G_LORA, d), F32)],
                               axis=0).astype(BF16),
    }
    if l > 0:
        lp["v0"] = v0[l - 1].reshape(1, d)
        lp["vup"] = jnp.concatenate([jnp.zeros((G_LORA, d), F32), v_up[l - 1],
                                     jnp.zeros((256 - G_LORA - V_LORA, d), F32)],
                                    axis=0).astype(BF16)
    return lp


def kernel(x, pre_mix_g, post_mix_g, pre_mlp_g, post_mlp_g, w_in, w_in_vres, mu, mu_vres, w0, w_up,
           a0, a_up, g_up, v0, v_up, k_k, k_a, r_k, lnx_w, lnx_b, sb_out_g, w_out, w_ff_up,
           w_ff_down):
    bsz, seq, d_model = x.shape
    depth = w_in.shape[0]
    n = bsz * seq
    assert w_in.shape[2] == 3 * D_RWKV + W_LORA + A_LORA + G_LORA + 3 * D_SB
    tm_mm = min(1024, n)
    tm_res = min(512, n)
    xf = x.reshape(n, d_model)
    h = _norm_cast(xf, pre_mix_g[0], tm=tm_res)
    vfirst = None
    for l in range(depth):
        lp = _layer_params(l, d_model, w_in, w_in_vres, mu, mu_vres, w0, w_up, a0, a_up, g_up,
                           v0, v_up, k_k, k_a)
        proj = _matmul(h, lp["w_rwkv"], relu2=False, tm=tm_mm, tn=P_RWKV // 2, name="proj_rwkv")
        qkv = _matmul(h, lp["w_sb"], relu2=False, tm=tm_mm, tn=1536, name="proj_sb")
        y_r, v_l = _rwkv_mix(proj, vfirst, lp, r_k[l], lnx_w[l], lnx_b[l],
                             bsz=bsz, seq=seq, tt=min(256, seq))
        if l == 0:
            vfirst = v_l
        y_s = _sb_attention(qkv, sb_out_g[l], bsz=bsz, seq=seq, blk=min(256, seq), ppg=4)
        xf, h = _mix_out(y_r, y_s, w_out[l].astype(BF16), xf, post_mix_g[l], pre_mlp_g[l],
                         tm=tm_res)
        ff = _matmul(h, w_ff_up[l].astype(BF16), relu2=True, tm=tm_mm, tn=2048, name="ffn_up")
        g_next = pre_mix_g[l + 1] if l + 1 < depth else None
        xf, h = _ffn_down(ff, w_ff_down[l].astype(BF16), xf, post_mlp_g[l], g_next,
                          tm=tm_res, tk=2048)
    return xf.reshape(bsz, seq, d_model)
```

```python
import functools

import jax
import jax.numpy as jnp
from jax import lax
from jax.experimental import pallas as pl
from jax.experimental.pallas import tpu as pltpu

F32 = jnp.float32
BF16 = jnp.bfloat16

HEAD_DIM = 64
LANES = 128
NORM_EPS = 1e-6
LNX_EPS = 64e-5
CHUNK = 64
VMEM_LIMIT = 48 * 1024 * 1024
LOG2E = 1.4426950408889634
DECAY_SCALE = 0.6065306597126334

D_RWKV = 1024
D_SB = 1024
W_LORA, A_LORA, V_LORA, G_LORA = 64, 64, 32, 160
SEG_A = 0
SEG_C = 3 * D_RWKV
SEG_B = SEG_C + 256
P_RWKV = 3584

NT_DIMS = (((1,), (1,)), ((), ()))


def _cparams(sem):
    return pltpu.CompilerParams(dimension_semantics=sem, vmem_limit_bytes=VMEM_LIMIT)


def _bdot(a, b):
    return jnp.dot(a.astype(BF16), b.astype(BF16), preferred_element_type=F32)


def _split2_dot(a_exact, x):
    hi = x.astype(BF16)
    lo = (x - hi.astype(F32)).astype(BF16)
    return (jnp.dot(a_exact, hi, preferred_element_type=F32)
            + jnp.dot(a_exact, lo, preferred_element_type=F32))


def _head_group_matrix(scale):
    i = lax.broadcasted_iota(jnp.int32, (LANES, LANES), 0) // HEAD_DIM
    j = lax.broadcasted_iota(jnp.int32, (LANES, LANES), 1) // HEAD_DIM
    return jnp.where(i == j, scale, 0.0).astype(BF16)


def _sigmoid(x):
    return 0.5 * jnp.tanh(0.5 * x) + 0.5


def _neg_abs(x):
    bits = lax.bitcast_convert_type(x, jnp.uint32) | jnp.uint32(0x80000000)
    return lax.bitcast_convert_type(bits, F32)


def _softplus2(z2):
    return jnp.maximum(z2, 0.0) + jnp.log2(1.0 + jnp.exp2(_neg_abs(z2)))


def _stacked_dot(xs, w):
    rows = xs[0].shape[0]
    out = _bdot(jnp.concatenate(xs, axis=0), w)
    return [out[i * rows:(i + 1) * rows] for i in range(len(xs))]


def _each(fn, *lists):
    return [fn(*xs) for xs in zip(*lists)]


def _rms_scaled(x, g):
    ms = jnp.mean(x * x, axis=-1, keepdims=True)
    return x * lax.rsqrt(ms + NORM_EPS) * g


def _norm_cast_kernel(x_ref, g_ref, h_ref):
    h_ref[...] = _rms_scaled(x_ref[...], g_ref[...]).astype(h_ref.dtype)


def _norm_cast(x, g, *, tm):
    n, d = x.shape
    return pl.pallas_call(
        _norm_cast_kernel,
        grid=(n // tm,),
        in_specs=[pl.BlockSpec((tm, d), lambda i: (i, 0)), pl.BlockSpec((1, d), lambda i: (0, 0))],
        out_specs=pl.BlockSpec((tm, d), lambda i: (i, 0)),
        out_shape=jax.ShapeDtypeStruct((n, d), BF16),
        compiler_params=_cparams(("parallel",)),
        name="norm_cast",
    )(x, g.reshape(1, d))


def _matmul_kernel(a_ref, w_ref, o_ref, *, relu2):
    y = jnp.dot(a_ref[...], w_ref[...], preferred_element_type=F32)
    if relu2:
        y = jnp.square(jnp.maximum(y, 0.0))
    o_ref[...] = y.astype(o_ref.dtype)


def _matmul(a, w, layer, *, relu2, tm, tn, name):
    n, kdim = a.shape
    p = w.shape[2]
    return pl.pallas_call(
        functools.partial(_matmul_kernel, relu2=relu2),
        grid=(n // tm, p // tn),
        in_specs=[
            pl.BlockSpec((tm, kdim), lambda i, j: (i, 0)),
            pl.BlockSpec((None, kdim, tn), lambda i, j: (layer, 0, j)),
        ],
        out_specs=pl.BlockSpec((tm, tn), lambda i, j: (i, j)),
        out_shape=jax.ShapeDtypeStruct((n, p), BF16),
        compiler_params=_cparams(("parallel", "arbitrary")),
        name=name,
    )(a, w)


def _residual_epilogue(acc, x_ref, g_ref, gn_ref, o_ref, h_ref):
    xn = x_ref[...] + _rms_scaled(acc, g_ref[...])
    o_ref[...] = xn
    if h_ref is not None:
        h_ref[...] = _rms_scaled(xn, gn_ref[...]).astype(h_ref.dtype)


def _mix_out_kernel(a1_ref, a2_ref, w1_ref, w2_ref, x_ref, g_ref, gn_ref, o_ref, h_ref):
    acc = jnp.dot(a1_ref[...], w1_ref[...], preferred_element_type=F32)
    acc = acc + jnp.dot(a2_ref[...], w2_ref[...], preferred_element_type=F32)
    _residual_epilogue(acc, x_ref, g_ref, gn_ref, o_ref, h_ref)


def _mix_out(a1, a2, w, layer, x, g, g_next, *, tm):
    n, d = x.shape
    k1 = a1.shape[1]
    k2 = a2.shape[1]
    assert k1 == k2 and w.shape[1] == k1 + k2
    return pl.pallas_call(
        _mix_out_kernel,
        grid=(n // tm,),
        in_specs=[
            pl.BlockSpec((tm, k1), lambda i: (i, 0)),
            pl.BlockSpec((tm, k2), lambda i: (i, 0)),
            pl.BlockSpec((None, k1, d), lambda i: (layer, 0, 0)),
            pl.BlockSpec((None, k2, d), lambda i: (layer, 1, 0)),
            pl.BlockSpec((tm, d), lambda i: (i, 0)),
            pl.BlockSpec((1, d), lambda i: (0, 0)),
            pl.BlockSpec((1, d), lambda i: (0, 0)),
        ],
        out_specs=[pl.BlockSpec((tm, d), lambda i: (i, 0))] * 2,
        out_shape=[jax.ShapeDtypeStruct((n, d), F32), jax.ShapeDtypeStruct((n, d), BF16)],
        compiler_params=_cparams(("parallel",)),
        name="mix_out",
    )(a1, a2, w, w, x, g.reshape(1, d), g_next.reshape(1, d))


def _ffn_down_kernel(*refs, nk, emit_h):
    if emit_h:
        a_ref, w_ref, x_ref, g_ref, gn_ref, o_ref, h_ref = refs
    else:
        a_ref, w_ref, x_ref, g_ref, o_ref = refs
        gn_ref = h_ref = None
    k = pl.program_id(1)
    part = jnp.dot(a_ref[...], w_ref[...], preferred_element_type=F32)

    @pl.when(k == 0)
    def _():
        o_ref[...] = part

    @pl.when(k > 0)
    def _():
        o_ref[...] += part

    @pl.when(k == nk - 1)
    def _():
        _residual_epilogue(o_ref[...], x_ref, g_ref, gn_ref, o_ref, h_ref)


def _ffn_down(a, w, layer, x, g, g_next, *, tm, tk):
    n, d = x.shape
    kdim = a.shape[1]
    nk = kdim // tk
    emit_h = g_next is not None
    row = pl.BlockSpec((tm, d), lambda i, k: (i, 0))
    gain = pl.BlockSpec((1, d), lambda i, k: (0, 0))
    in_specs = [pl.BlockSpec((tm, tk), lambda i, k: (i, k)),
                pl.BlockSpec((None, tk, d), lambda i, k: (layer, k, 0)), row, gain]
    args = [a, w, x, g.reshape(1, d)]
    out_specs = [row]
    out_shape = [jax.ShapeDtypeStruct((n, d), F32)]
    if emit_h:
        in_specs.append(gain)
        args.append(g_next.reshape(1, d))
        out_specs.append(row)
        out_shape.append(jax.ShapeDtypeStruct((n, d), BF16))
    outs = pl.pallas_call(
        functools.partial(_ffn_down_kernel, nk=nk, emit_h=emit_h),
        grid=(n // tm, nk),
        in_specs=in_specs,
        out_specs=out_specs,
        out_shape=out_shape,
        compiler_params=_cparams(("parallel", "arbitrary")),
        name="ffn_down",
    )(*args)
    return (outs[0], outs[1]) if emit_h else (outs[0], None)


def _prep_compute(in_refs, out_refs, carry_refs, *, has_vres, tt):
    (pa_ref, pc_ref, pb_ref, mua_ref, muc_ref, mub_ref, w0_ref, wup_ref, a0_ref, aup_ref,
     gup_ref, kk_ref, ka_ref) = in_refs[0:13]
    if has_vres:
        vf_ref, v0_ref, vup_ref = in_refs[13:16]
    r_out, k_out, v_out, lw_out, kk_out, a_out, g_out = out_refs
    ca_ref, cc_ref, cb_ref = carry_refs

    @pl.when(pl.program_id(1) == 0)
    def _():
        ca_ref[...] = jnp.zeros_like(ca_ref)
        cc_ref[...] = jnp.zeros_like(cc_ref)
        cb_ref[...] = jnp.zeros_like(cb_ref)

    down = (lax.broadcasted_iota(jnp.int32, (tt, tt), 1) + 1
            == lax.broadcasted_iota(jnp.int32, (tt, tt), 0)).astype(BF16)

    def shifted(pb16, prev_row, mu):
        p = pb16.astype(F32)
        prev = jnp.dot(down, pb16, preferred_element_type=F32)
        row = lax.broadcasted_iota(jnp.int32, p.shape, 0)
        prev = jnp.where(row == 0, prev_row, prev)
        return p + (prev - p) * mu

    xb = shifted(pb_ref[...], cb_ref[7:8, :], mub_ref[...])
    xc = shifted(pc_ref[...], cc_ref[7:8, :], muc_ref[...])

    wl = w0_ref[...] + _bdot(jnp.tanh(xb), wup_ref[...])
    lw_out[...] = -DECAY_SCALE * _sigmoid(wl)
    a = _sigmoid(a0_ref[...] + _bdot(xb, aup_ref[...]))
    a_out[...] = a.astype(a_out.dtype)
    g_out[...] = _bdot(_sigmoid(xc), gup_ref[...]).astype(g_out.dtype)

    d = D_RWKV
    r_out[...] = shifted(pa_ref[:, 0:d], ca_ref[7:8, 0:d], mua_ref[:, 0:d]).astype(r_out.dtype)
    k = shifted(pa_ref[:, d:2 * d], ca_ref[7:8, d:2 * d], mua_ref[:, d:2 * d])
    kk_out[...] = (k * kk_ref[...]).astype(kk_out.dtype)
    k_out[...] = (k * (1.0 + (a - 1.0) * ka_ref[...])).astype(k_out.dtype)
    v = shifted(pa_ref[:, 2 * d:3 * d], ca_ref[7:8, 2 * d:3 * d], mua_ref[:, 2 * d:3 * d])
    if has_vres:
        mix = _sigmoid(v0_ref[...] + _bdot(xc, vup_ref[...]))
        v = v + (vf_ref[...].astype(F32) - v) * mix
    v_out[...] = v.astype(v_out.dtype)

    ca_ref[...] = pa_ref[tt - 8:tt, :].astype(F32)
    cc_ref[...] = pc_ref[tt - 8:tt, :].astype(F32)
    cb_ref[...] = pb_ref[tt - 8:tt, :].astype(F32)


def _unit_lower_inverses(lps, d16):
    n = lps[0].shape[0]
    eye = (lax.broadcasted_iota(jnp.int32, (n, n), 0)
           == lax.broadcasted_iota(jnp.int32, (n, n), 1)).astype(F32)
    ld = [jnp.where(d16, lp, 0.0) for lp in lps]
    e = _each(lambda a, b: a - b, lps, ld)
    l2 = _each(_bdot, ld, ld)
    p = [eye + x for x in ld]
    p = _each(lambda a, b: a + _bdot(a, b), p, l2)
    l4 = _each(_bdot, l2, l2)
    p = _each(lambda a, b: a + _bdot(a, b), p, l4)
    l8 = _each(_bdot, l4, l4)
    dinv = _each(lambda a, b: a + _bdot(a, b), p, l8)
    nm = _each(_bdot, dinv, e)
    n2 = _each(_bdot, nm, nm)
    a1 = _each(lambda a, b: a + _bdot(b, a), dinv, nm)
    return _each(lambda a, b: a + _bdot(b, a), a1, n2)


def _scan_compute(r_ref, k_ref, v_ref, lw_ref, kk_ref, a_ref, g_ref, rk_ref, lnw_ref, lnb_ref,
                  y_ref, s_ref, *, tt, npair):
    c = CHUNK
    nch = tt // c

    @pl.when(pl.program_id(1) == 0)
    def _():
        s_ref[...] = jnp.zeros_like(s_ref)

    row = lax.broadcasted_iota(jnp.int32, (c, LANES), 0)
    lane = lax.broadcasted_iota(jnp.int32, (c, LANES), 1)
    is_a = lane < HEAD_DIM
    s_idx = lane % HEAD_DIM
    strict = s_idx < row
    incl = s_idx <= row
    strict_a = is_a & strict
    strict_b = (~is_a) & strict
    r2 = lax.broadcasted_iota(jnp.int32, (2 * c, LANES), 0)
    l2 = lax.broadcasted_iota(jnp.int32, (2 * c, LANES), 1)
    d16 = (r2 // 16) == (l2 // 16)
    same_head = (r2 // HEAD_DIM) == (l2 // HEAD_DIM)
    tri_incl = (lax.broadcasted_iota(jnp.int32, (c, c), 1)
                <= lax.broadcasted_iota(jnp.int32, (c, c), 0)).astype(BF16)
    gsum = _head_group_matrix(1.0)

    def head_stack(x):
        return jnp.concatenate([jnp.where(is_a, x, 0.0), jnp.where(is_a, 0.0, x)], axis=0)

    def head_stack_swapped(x):
        return jnp.concatenate([jnp.where(is_a, 0.0, x), jnp.where(is_a, x, 0.0)], axis=0)

    elems = [(ci, p) for ci in range(nch) for p in range(npair)]

    def tiles(ref):
        return [ref[ci * c:(ci + 1) * c, p * LANES:(p + 1) * LANES].astype(F32)
                for ci, p in elems]

    kkr = tiles(kk_ref)
    lw = tiles(lw_ref)
    ss = _stacked_dot([x * x for x in kkr], gsum)
    c_in = [_split2_dot(tri_incl, x) for x in lw]
    kk = _each(lambda x, s: x * lax.rsqrt(jnp.maximum(s, 1e-24)), kkr, ss)
    b = _each(lambda x, a: x * a, kk, tiles(a_ref))
    e_tot = [jnp.exp(x[c - 1:c, :]) for x in c_in]
    e_neg = [jnp.exp(-x) for x in c_in]
    at = _each(lambda x, ci_, l: -x * jnp.exp(ci_ - l), kk, c_in, lw)
    rt = _each(lambda x, ci_: x * jnp.exp(ci_), tiles(r_ref), c_in)
    bt = _each(lambda x, en: x * en, b, e_neg)
    kt = _each(lambda x, en: x * en, tiles(k_ref), e_neg)
    v = tiles(v_ref)
    ar = _each(lambda x, y: jnp.concatenate([x, y], axis=0).astype(BF16), at, rt)
    bkt = _each(lambda x, y: jnp.concatenate([x, y], axis=0).astype(BF16), bt, kt)
    lhs = [jnp.concatenate([jnp.where(l2 < HEAD_DIM, x, jnp.zeros_like(x)),
                            jnp.where(l2 < HEAD_DIM, jnp.zeros_like(x), x)], axis=0) for x in ar]
    g_ab = _each(lambda x, y: lax.dot_general(x, y, NT_DIMS, preferred_element_type=F32),
                 lhs, bkt)
    g_a = [x[0:2 * c] for x in g_ab]
    g_b = [pltpu.roll(x[2 * c:4 * c], HEAD_DIM, axis=1) for x in g_ab]
    lps = _each(lambda ga, gb: jnp.concatenate([jnp.where(strict_a, ga[0:c], 0.0),
                                                jnp.where(strict_b, gb[0:c], 0.0)], axis=0),
                g_a, g_b)
    lmk = _each(lambda ga, gb: jnp.concatenate(
        [jnp.where(strict, jnp.where(is_a, gb[0:c], ga[0:c]), 0.0),
         jnp.where(incl, jnp.where(is_a, gb[c:2 * c], ga[c:2 * c]), 0.0)], axis=0), g_a, g_b)
    mrb = _each(lambda ga, gb: jnp.where(incl, jnp.where(is_a, ga[c:2 * c], gb[c:2 * c]),
                                         0.0).astype(BF16), g_a, g_b)
    lv = _each(lambda m, x: _bdot(m, head_stack_swapped(x)), lmk, v)
    bk_h = _each(lambda x, y, et: jnp.concatenate([x * et, y * et], axis=0).astype(BF16),
                 bt, kt, e_tot)
    tps = [x.astype(BF16) for x in _unit_lower_inverses(lps, d16)]

    s = [s_ref[p] for p in range(npair)]
    ys = []
    for ci in range(nch):
        idx = [ci * npair + p for p in range(npair)]
        ah = [lax.dot_general(ar[i], s[p].astype(BF16), NT_DIMS, preferred_element_type=F32)
              for p, i in enumerate(idx)]
        x = [ah[p][0:c] + lv[i][0:c] for p, i in enumerate(idx)]
        tu = [jnp.dot(tps[i], head_stack(x[p]).astype(BF16), preferred_element_type=F32)
              for p, i in enumerate(idx)]
        u = [t[0:c] + t[c:2 * c] for t in tu]
        ys.append([ah[p][c:2 * c] + lv[i][c:2 * c]
                   + jnp.dot(mrb[i], head_stack(u[p]).astype(BF16), preferred_element_type=F32)
                   for p, i in enumerate(idx)])
        uv_t = [jnp.concatenate([u[p], v[i]], axis=0).T.astype(BF16)
                for p, i in enumerate(idx)]
        s = [s[p] * e_tot[i]
             + jnp.where(same_head, jnp.dot(uv_t[p], bk_h[i], preferred_element_type=F32), 0.0)
             for p, i in enumerate(idx)]
    for p in range(npair):
        s_ref[p] = s[p]

    gavg = _head_group_matrix(1.0 / HEAD_DIM)
    cols = [slice(p * LANES, (p + 1) * LANES) for p in range(npair)]
    y = [jnp.concatenate([ys[ci][p] for ci in range(nch)], axis=0) for p in range(npair)]
    rk = [r_ref[:, cs].astype(F32) * k_ref[:, cs].astype(F32) * rk_ref[:, cs] for cs in cols]
    mean = _stacked_dot(y, gavg)
    rk_sum = _stacked_dot(rk, gsum)
    yc = _each(lambda a, b: a - b, y, mean)
    var = _stacked_dot([x * x for x in yc], gavg)
    for p, cs in enumerate(cols):
        yn = yc[p] * lax.rsqrt(var[p] + LNX_EPS) * lnw_ref[:, cs] + lnb_ref[:, cs]
        bonus = rk_sum[p] * v_ref[:, cs].astype(F32)
        y_ref[:, cs] = ((yn + bonus) * g_ref[:, cs].astype(F32)).astype(y_ref.dtype)


def _rwkv_mix_kernel(*refs, tt, npair, has_vres):
    n_prep = 16 if has_vres else 13
    prep_in = refs[0:n_prep]
    rk_ref, lnw_ref, lnb_ref = refs[n_prep:n_prep + 3]
    n_out = 1 if has_vres else 2
    outs = refs[n_prep + 3:n_prep + 3 + n_out]
    s_ref, ca_ref, cc_ref, cb_ref = refs[n_prep + 3 + n_out:n_prep + 7 + n_out]
    work = refs[n_prep + 7 + n_out:]
    _prep_compute(prep_in, work, (ca_ref, cc_ref, cb_ref), has_vres=has_vres, tt=tt)
    if not has_vres:
        outs[1][...] = work[2][...].astype(outs[1].dtype)
    _scan_compute(*work, rk_ref, lnw_ref, lnb_ref, outs[0], s_ref, tt=tt, npair=npair)


def _rwkv_mix(proj, vfirst, lp, r_k, lnx_w, lnx_b, *, bsz, seq, tt):
    n = bsz * seq
    nt = seq // tt
    has_vres = vfirst is not None
    d = D_RWKV
    npair = d // LANES
    row = lambda b, t: (b * nt + t, 0)
    const = lambda b, t: (0, 0)
    rowblk = pl.BlockSpec((tt, d), row)
    par = pl.BlockSpec((1, d), const)
    in_specs = [
        pl.BlockSpec((tt, 3 * d), lambda b, t: (b * nt + t, SEG_A // (3 * d))),
        pl.BlockSpec((tt, 256), lambda b, t: (b * nt + t, SEG_C // 256)),
        pl.BlockSpec((tt, 128), lambda b, t: (b * nt + t, SEG_B // 128)),
        pl.BlockSpec((1, 3 * d), const),
        pl.BlockSpec((1, 256), const),
        pl.BlockSpec((1, 128), const),
        par,
        pl.BlockSpec((128, d), const),
        par,
        pl.BlockSpec((128, d), const),
        pl.BlockSpec((256, d), const),
        par,
        par,
    ]
    args = [proj, proj, proj, lp["mu_a"], lp["mu_c"], lp["mu_b"], lp["w0"], lp["wup"], lp["a0"],
            lp["aup"], lp["gup"], lp["k_k"], lp["k_a"]]
    if has_vres:
        in_specs += [rowblk, par, pl.BlockSpec((256, d), const)]
        args += [vfirst, lp["v0"], lp["vup"]]
    in_specs += [par] * 3
    args += [r_k.reshape(1, d), lnx_w.reshape(1, d), lnx_b.reshape(1, d)]
    n_out = 1 if has_vres else 2
    outs = pl.pallas_call(
        functools.partial(_rwkv_mix_kernel, tt=tt, npair=npair, has_vres=has_vres),
        grid=(bsz, nt),
        in_specs=in_specs,
        out_specs=[rowblk] * n_out,
        out_shape=[jax.ShapeDtypeStruct((n, d), BF16)] * n_out,
        scratch_shapes=[pltpu.VMEM((npair, LANES, LANES), F32),
                        pltpu.VMEM((8, 3 * d), F32), pltpu.VMEM((8, 256), F32),
                        pltpu.VMEM((8, 128), F32)] + [pltpu.VMEM((tt, d), F32)] * 7,
        compiler_params=_cparams(("parallel", "arbitrary")),
        name="rwkv_mix",
    )(*args)
    return (outs[0], None) if has_vres else (outs[0], outs[1])


def _sb_kernel(q_ref, k_ref, v_ref, g_ref, o_ref, vab_ref, *, blk, nblk, ppg, scale):
    i = pl.program_id(2)
    lane = lax.broadcasted_iota(jnp.int32, (blk, LANES), 1)
    is_a = lane < HEAD_DIM
    nh = 2 * ppg
    cols = [slice(p * LANES, (p + 1) * LANES) for p in range(ppg)]

    @pl.when(i == 0)
    def _():
        for p in range(ppg):
            for j in range(nblk):
                vb = v_ref[j * blk:(j + 1) * blk, cols[p]]
                vab_ref[p, j, 0:blk, :] = jnp.where(is_a, vb, jnp.zeros_like(vb))
                vab_ref[p, j, blk:2 * blk, :] = jnp.where(is_a, jnp.zeros_like(vb), vb)

    tq = lax.broadcasted_iota(jnp.int32, (blk, blk), 0)
    sk = lax.broadcasted_iota(jnp.int32, (blk, blk), 1)
    causal = sk < tq
    suffix = (tq > sk).astype(BF16)

    q_heads = []
    for p in range(ppg):
        q = q_ref[:, cols[p]].astype(F32) * (scale * LOG2E)
        q_heads += [jnp.where(is_a, q, 0.0).astype(BF16), jnp.where(is_a, 0.0, q).astype(BF16)]

    def block_steps(js, state, diag):
        carries = list(state[0:nh])
        accs = list(state[nh:nh + ppg])
        kbs = [[k_ref[pl.ds(pl.multiple_of(j * blk, blk), blk), cols[p]] for p in range(ppg)]
               for j in js]
        z2 = [[lax.dot_general(q_heads[h], kb[h // 2], NT_DIMS, preferred_element_type=F32)
               for h in range(nh)] for kb in kbs]
        if diag:
            z2[0] = [jnp.where(causal, z, -jnp.inf) for z in z2[0]]
        sp = [[_softplus2(z) for z in zs] for zs in z2]
        spb = [[s.astype(BF16) for s in ss] for ss in sp]
        lsig = [_each(lambda z, s: z - s, zs, ss) for zs, ss in zip(z2, sp)]
        if diag:
            between = [[jnp.dot(s, suffix, preferred_element_type=F32) for s in ss] for ss in spb]
        else:
            bt_all = jnp.dot(jnp.concatenate([s for ss in spb for s in ss], axis=0), suffix,
                             preferred_element_type=F32)
            between = [[bt_all[(b * nh + h) * blk:(b * nh + h + 1) * blk] for h in range(nh)]
                       for b in range(len(js))]
        for b, j in enumerate(js):
            atts = []
            for h in range(nh):
                atts.append(jnp.exp2((lsig[b][h] - between[b][h] - carries[h]).astype(BF16)))
                carries[h] = carries[h] + between[b][h][:, 0:1] + sp[b][h][:, 0:1]
            for p in range(ppg):
                accs[p] = accs[p] + jnp.dot(jnp.concatenate(atts[2 * p:2 * p + 2], axis=1),
                                            vab_ref[p, j], preferred_element_type=F32)
        return tuple(carries) + tuple(accs)

    state = (jnp.zeros((blk, 1), F32),) * nh + (jnp.zeros((blk, LANES), F32),) * ppg
    first = jnp.where(i == 0, 0, 2 - i % 2)
    state = lax.switch(first, [lambda s: block_steps([i], s, True),
                               lambda s: block_steps([i, i - 1], s, True),
                               lambda s: block_steps([i, i - 1, i - 2], s, True)], state)

    def body(m, s):
        j = i - 1 - first - 2 * m
        return block_steps([j, j - 1], s, False)

    state = lax.fori_loop(0, (i - first) // 2, body, state)

    gavg = _head_group_matrix(1.0 / HEAD_DIM)
    accs = [state[nh + p] for p in range(ppg)]
    ms = [_bdot(acc * acc, gavg) for acc in accs]
    for p in range(ppg):
        o_ref[:, cols[p]] = (accs[p] * lax.rsqrt(ms[p] + NORM_EPS)
                             * g_ref[:, cols[p]]).astype(o_ref.dtype)


def _sb_attention(qkv, gain, *, bsz, seq, blk, ppg):
    n = bsz * seq
    nq = seq // blk
    ngrp = D_SB // (LANES * ppg)
    w = LANES * ppg
    return pl.pallas_call(
        functools.partial(_sb_kernel, blk=blk, nblk=nq, ppg=ppg, scale=1.0 / (HEAD_DIM ** 0.5)),
        grid=(bsz, ngrp, nq),
        in_specs=[
            pl.BlockSpec((blk, w), lambda b, h, i: (b * nq + i, h)),
            pl.BlockSpec((seq, w), lambda b, h, i: (b, ngrp + h)),
            pl.BlockSpec((seq, w), lambda b, h, i: (b, 2 * ngrp + h)),
            pl.BlockSpec((1, w), lambda b, h, i: (0, h)),
        ],
        out_specs=pl.BlockSpec((blk, w), lambda b, h, i: (b * nq + i, h)),
        out_shape=jax.ShapeDtypeStruct((n, D_SB), BF16),
        scratch_shapes=[pltpu.VMEM((ppg, nq, 2 * blk, LANES), BF16)],
        compiler_params=_cparams(("parallel", "parallel", "arbitrary")),
        name="sb_attention",
    )(qkv, qkv, qkv, gain.reshape(1, D_SB))


def _layer_params(l, d_model, w_in, w_in_vres, mu, mu_vres, w0, w_up, a0, a_up, g_up, v0, v_up,
                  k_k, k_a):
    d = D_RWKV
    wl = w_in[l]
    c_dw = 3 * d
    c_da = c_dw + W_LORA
    c_dg = c_da + A_LORA
    c_sb = c_dg + G_LORA
    zcol = lambda m: jnp.zeros((d_model, m), F32)
    zrow = lambda m: jnp.zeros((m,), F32)
    if l > 0:
        vres_w, vres_mu = w_in_vres[l - 1], mu_vres[l - 1]
    else:
        vres_w, vres_mu = zcol(V_LORA), zrow(V_LORA)
    w_rwkv = jnp.concatenate([
        wl[:, 0:c_dw],
        wl[:, c_dg:c_sb], vres_w, zcol(256 - G_LORA - V_LORA),
        wl[:, c_dw:c_dg],
        zcol(P_RWKV - SEG_B - 128),
    ], axis=1).astype(BF16)
    mul = mu[l]
    lp = {
        "w_rwkv": w_rwkv,
        "w_sb": wl[:, c_sb:].astype(BF16),
        "mu_a": mul[0:c_dw].reshape(1, -1),
        "mu_c": jnp.concatenate([mul[c_dg:c_sb], vres_mu,
                                 zrow(256 - G_LORA - V_LORA)]).reshape(1, -1),
        "mu_b": mul[c_dw:c_dg].reshape(1, -1),
        "w0": w0[l].reshape(1, d),
        "a0": a0[l].reshape(1, d),
        "k_k": k_k[l].reshape(1, d),
        "k_a": k_a[l].reshape(1, d),
        "wup": jnp.concatenate([w_up[l], jnp.zeros((A_LORA, d), F32)], axis=0).astype(BF16),
        "aup": jnp.concatenate([jnp.zeros((W_LORA, d), F32), a_up[l]], axis=0).astype(BF16),
        "gup": jnp.concatenate([g_up[l], jnp.zeros((256 - G_LORA, d), F32)],
                               axis=0).astype(BF16),
    }
    if l > 0:
        lp["v0"] = v0[l - 1].reshape(1, d)
        lp["vup"] = jnp.concatenate([jnp.zeros((G_LORA, d), F32), v_up[l - 1],
                                     jnp.zeros((256 - G_LORA - V_LORA, d), F32)],
                                    axis=0).astype(BF16)
    return lp


def kernel(x, pre_mix_g, post_mix_g, pre_mlp_g, post_mlp_g, w_in, w_in_vres, mu, mu_vres, w0, w_up,
           a0, a_up, g_up, v0, v_up, k_k, k_a, r_k, lnx_w, lnx_b, sb_out_g, w_out, w_ff_up,
           w_ff_down):
    bsz, seq, d_model = x.shape
    depth = w_in.shape[0]
    n = bsz * seq
    assert w_in.shape[2] == 3 * D_RWKV + W_LORA + A_LORA + G_LORA + 3 * D_SB
    tm_mm = min(1024, n)
    tm_res = min(512, n)
    xf = x.reshape(n, d_model)
    h = _norm_cast(xf, pre_mix_g[0], tm=tm_res)
    w_out_b, w_up_b, w_down_b = (w.astype(BF16) for w in (w_out, w_ff_up, w_ff_down))
    vfirst = None
    for l in range(depth):
        lp = _layer_params(l, d_model, w_in, w_in_vres, mu, mu_vres, w0, w_up, a0, a_up, g_up,
                           v0, v_up, k_k, k_a)
        proj = _matmul(h, lp["w_rwkv"][None], 0, relu2=False, tm=tm_mm, tn=P_RWKV // 2,
                       name="proj_rwkv")
        qkv = _matmul(h, lp["w_sb"][None], 0, relu2=False, tm=tm_mm, tn=1536, name="proj_sb")
        y_r, v_l = _rwkv_mix(proj, vfirst, lp, r_k[l], lnx_w[l], lnx_b[l],
                             bsz=bsz, seq=seq, tt=min(256, seq))
        if l == 0:
            vfirst = v_l
        y_s = _sb_attention(qkv, sb_out_g[l], bsz=bsz, seq=seq, blk=min(256, seq), ppg=4)
        xf, h = _mix_out(y_r, y_s, w_out_b, l, xf, post_mix_g[l], pre_mlp_g[l], tm=tm_res)
        ff = _matmul(h, w_up_b, l, relu2=True, tm=tm_mm, tn=2048, name="ffn_up")
        g_next = pre_mix_g[l + 1] if l + 1 < depth else None
        xf, h = _ffn_down(ff, w_down_b, l, xf, post_mlp_g[l], g_next, tm=tm_res, tk=2048)
    return xf.reshape(bsz, seq, d_model)
```

```python
import functools

import jax
import jax.numpy as jnp
from jax import lax
from jax.experimental import pallas as pl
from jax.experimental.pallas import tpu as pltpu

F32 = jnp.float32
BF16 = jnp.bfloat16

HEAD_DIM = 64
LANES = 128
NORM_EPS = 1e-6
LNX_EPS = 64e-5
CHUNK = 64
VMEM_LIMIT = 48 * 1024 * 1024
LOG2E = 1.4426950408889634
DECAY_SCALE = 0.6065306597126334

D_RWKV = 1024
D_SB = 1024
W_LORA, A_LORA, V_LORA, G_LORA = 64, 64, 32, 160
SEG_A = 0
SEG_C = 3 * D_RWKV
SEG_B = SEG_C + 256
P_RWKV = 3584

NT_DIMS = (((1,), (1,)), ((), ()))


def _cparams(sem):
    return pltpu.CompilerParams(dimension_semantics=sem, vmem_limit_bytes=VMEM_LIMIT)


def _bdot(a, b):
    return jnp.dot(a.astype(BF16), b.astype(BF16), preferred_element_type=F32)


def _split2_dot(a_exact, x):
    hi = x.astype(BF16)
    lo = (x - hi.astype(F32)).astype(BF16)
    return (jnp.dot(a_exact, hi, preferred_element_type=F32)
            + jnp.dot(a_exact, lo, preferred_element_type=F32))


def _head_group_matrix(scale):
    i = lax.broadcasted_iota(jnp.int32, (LANES, LANES), 0) // HEAD_DIM
    j = lax.broadcasted_iota(jnp.int32, (LANES, LANES), 1) // HEAD_DIM
    return jnp.where(i == j, scale, 0.0).astype(BF16)


def _sigmoid(x):
    return 0.5 * jnp.tanh(0.5 * x) + 0.5


def _neg_abs(x):
    bits = lax.bitcast_convert_type(x, jnp.uint32) | jnp.uint32(0x80000000)
    return lax.bitcast_convert_type(bits, F32)


def _softplus2(z2):
    return jnp.maximum(z2, 0.0) + jnp.log2(1.0 + jnp.exp2(_neg_abs(z2)))


def _stacked_dot(xs, w):
    rows = xs[0].shape[0]
    out = _bdot(jnp.concatenate(xs, axis=0), w)
    return [out[i * rows:(i + 1) * rows] for i in range(len(xs))]


def _each(fn, *lists):
    return [fn(*xs) for xs in zip(*lists)]


def _rms_scaled(x, g):
    ms = jnp.mean(x * x, axis=-1, keepdims=True)
    return x * lax.rsqrt(ms + NORM_EPS) * g


def _norm_cast_kernel(x_ref, g_ref, h_ref):
    h_ref[...] = _rms_scaled(x_ref[...], g_ref[...]).astype(h_ref.dtype)


def _norm_cast(x, g, *, tm):
    n, d = x.shape
    return pl.pallas_call(
        _norm_cast_kernel,
        grid=(n // tm,),
        in_specs=[pl.BlockSpec((tm, d), lambda i: (i, 0)), pl.BlockSpec((1, d), lambda i: (0, 0))],
        out_specs=pl.BlockSpec((tm, d), lambda i: (i, 0)),
        out_shape=jax.ShapeDtypeStruct((n, d), BF16),
        compiler_params=_cparams(("parallel",)),
        name="norm_cast",
    )(x, g.reshape(1, d))


def _matmul_kernel(a_ref, w_ref, o_ref, *, relu2):
    y = jnp.dot(a_ref[...], w_ref[...], preferred_element_type=F32)
    if relu2:
        y = jnp.square(jnp.maximum(y, 0.0))
    o_ref[...] = y.astype(o_ref.dtype)


def _matmul(a, w, layer, *, relu2, tm, tn, name):
    n, kdim = a.shape
    p = w.shape[2]
    return pl.pallas_call(
        functools.partial(_matmul_kernel, relu2=relu2),
        grid=(n // tm, p // tn),
        in_specs=[
            pl.BlockSpec((tm, kdim), lambda i, j: (i, 0)),
            pl.BlockSpec((None, kdim, tn), lambda i, j: (layer, 0, j)),
        ],
        out_specs=pl.BlockSpec((tm, tn), lambda i, j: (i, j)),
        out_shape=jax.ShapeDtypeStruct((n, p), BF16),
        compiler_params=_cparams(("parallel", "arbitrary")),
        name=name,
    )(a, w)


def _residual_epilogue(acc, x_ref, g_ref, gn_ref, o_ref, h_ref):
    xn = x_ref[...] + _rms_scaled(acc, g_ref[...])
    o_ref[...] = xn
    if h_ref is not None:
        h_ref[...] = _rms_scaled(xn, gn_ref[...]).astype(h_ref.dtype)


def _mix_out_kernel(a1_ref, a2_ref, w1_ref, w2_ref, x_ref, g_ref, gn_ref, o_ref, h_ref):
    acc = jnp.dot(a1_ref[...], w1_ref[...], preferred_element_type=F32)
    acc = acc + jnp.dot(a2_ref[...], w2_ref[...], preferred_element_type=F32)
    _residual_epilogue(acc, x_ref, g_ref, gn_ref, o_ref, h_ref)


def _mix_out(a1, a2, w, layer, x, g, g_next, *, tm):
    n, d = x.shape
    k1 = a1.shape[1]
    k2 = a2.shape[1]
    assert k1 == k2 and w.shape[1] == k1 + k2
    return pl.pallas_call(
        _mix_out_kernel,
        grid=(n // tm,),
        in_specs=[
            pl.BlockSpec((tm, k1), lambda i: (i, 0)),
            pl.BlockSpec((tm, k2), lambda i: (i, 0)),
            pl.BlockSpec((None, k1, d), lambda i: (layer, 0, 0)),
            pl.BlockSpec((None, k2, d), lambda i: (layer, 1, 0)),
            pl.BlockSpec((tm, d), lambda i: (i, 0)),
            pl.BlockSpec((1, d), lambda i: (0, 0)),
            pl.BlockSpec((1, d), lambda i: (0, 0)),
        ],
        out_specs=[pl.BlockSpec((tm, d), lambda i: (i, 0))] * 2,
        out_shape=[jax.ShapeDtypeStruct((n, d), F32), jax.ShapeDtypeStruct((n, d), BF16)],
        compiler_params=_cparams(("parallel",)),
        name="mix_out",
    )(a1, a2, w, w, x, g.reshape(1, d), g_next.reshape(1, d))


def _ffn_down_kernel(*refs, nk, emit_h):
    if emit_h:
        a_ref, w_ref, x_ref, g_ref, gn_ref, o_ref, h_ref = refs
    else:
        a_ref, w_ref, x_ref, g_ref, o_ref = refs
        gn_ref = h_ref = None
    k = pl.program_id(1)
    part = jnp.dot(a_ref[...], w_ref[...], preferred_element_type=F32)

    @pl.when(k == 0)
    def _():
        o_ref[...] = part

    @pl.when(k > 0)
    def _():
        o_ref[...] += part

    @pl.when(k == nk - 1)
    def _():
        _residual_epilogue(o_ref[...], x_ref, g_ref, gn_ref, o_ref, h_ref)


def _ffn_down(a, w, layer, x, g, g_next, *, tm, tk):
    n, d = x.shape
    kdim = a.shape[1]
    nk = kdim // tk
    emit_h = g_next is not None
    row = pl.BlockSpec((tm, d), lambda i, k: (i, 0))
    gain = pl.BlockSpec((1, d), lambda i, k: (0, 0))
    in_specs = [pl.BlockSpec((tm, tk), lambda i, k: (i, k)),
                pl.BlockSpec((None, tk, d), lambda i, k: (layer, k, 0)), row, gain]
    args = [a, w, x, g.reshape(1, d)]
    out_specs = [row]
    out_shape = [jax.ShapeDtypeStruct((n, d), F32)]
    if emit_h:
        in_specs.append(gain)
        args.append(g_next.reshape(1, d))
        out_specs.append(row)
        out_shape.append(jax.ShapeDtypeStruct((n, d), BF16))
    outs = pl.pallas_call(
        functools.partial(_ffn_down_kernel, nk=nk, emit_h=emit_h),
        grid=(n // tm, nk),
        in_specs=in_specs,
        out_specs=out_specs,
        out_shape=out_shape,
        compiler_params=_cparams(("parallel", "arbitrary")),
        name="ffn_down",
    )(*args)
    return (outs[0], outs[1]) if emit_h else (outs[0], None)


def _prep_compute(in_refs, out_refs, carry_refs, *, has_vres, tt):
    (pa_ref, pc_ref, pb_ref, mua_ref, muc_ref, mub_ref, w0_ref, wup_ref, a0_ref, aup_ref,
     gup_ref, kk_ref, ka_ref) = in_refs[0:13]
    if has_vres:
        vf_ref, v0_ref, vup_ref = in_refs[13:16]
    r_out, k_out, v_out, lw_out, kk_out, a_out, g_out = out_refs
    ca_ref, cc_ref, cb_ref = carry_refs

    @pl.when(pl.program_id(1) == 0)
    def _():
        ca_ref[...] = jnp.zeros_like(ca_ref)
        cc_ref[...] = jnp.zeros_like(cc_ref)
        cb_ref[...] = jnp.zeros_like(cb_ref)

    down = (lax.broadcasted_iota(jnp.int32, (tt, tt), 1) + 1
            == lax.broadcasted_iota(jnp.int32, (tt, tt), 0)).astype(BF16)

    def shifted(pb16, prev_row, mu):
        p = pb16.astype(F32)
        prev = jnp.dot(down, pb16, preferred_element_type=F32)
        row = lax.broadcasted_iota(jnp.int32, p.shape, 0)
        prev = jnp.where(row == 0, prev_row, prev)
        return p + (prev - p) * mu

    xb = shifted(pb_ref[...], cb_ref[7:8, :], mub_ref[...])
    xc = shifted(pc_ref[...], cc_ref[7:8, :], muc_ref[...])

    wl = w0_ref[...] + _bdot(jnp.tanh(xb), wup_ref[...])
    lw_out[...] = -DECAY_SCALE * _sigmoid(wl)
    a = _sigmoid(a0_ref[...] + _bdot(xb, aup_ref[...]))
    a_out[...] = a.astype(a_out.dtype)
    g_out[...] = _bdot(_sigmoid(xc), gup_ref[...]).astype(g_out.dtype)

    d = D_RWKV
    r_out[...] = shifted(pa_ref[:, 0:d], ca_ref[7:8, 0:d], mua_ref[:, 0:d]).astype(r_out.dtype)
    k = shifted(pa_ref[:, d:2 * d], ca_ref[7:8, d:2 * d], mua_ref[:, d:2 * d])
    kk_out[...] = (k * kk_ref[...]).astype(kk_out.dtype)
    k_out[...] = (k * (1.0 + (a - 1.0) * ka_ref[...])).astype(k_out.dtype)
    v = shifted(pa_ref[:, 2 * d:3 * d], ca_ref[7:8, 2 * d:3 * d], mua_ref[:, 2 * d:3 * d])
    if has_vres:
        mix = _sigmoid(v0_ref[...] + _bdot(xc, vup_ref[...]))
        v = v + (vf_ref[...].astype(F32) - v) * mix
    v_out[...] = v.astype(v_out.dtype)

    ca_ref[...] = pa_ref[tt - 8:tt, :].astype(F32)
    cc_ref[...] = pc_ref[tt - 8:tt, :].astype(F32)
    cb_ref[...] = pb_ref[tt - 8:tt, :].astype(F32)


def _unit_lower_inverses(lps, d16):
    n = lps[0].shape[0]
    eye = (lax.broadcasted_iota(jnp.int32, (n, n), 0)
           == lax.broadcasted_iota(jnp.int32, (n, n), 1)).astype(F32)
    ld = [jnp.where(d16, lp, 0.0) for lp in lps]
    e = _each(lambda a, b: a - b, lps, ld)
    l2 = _each(_bdot, ld, ld)
    p = [eye + x for x in ld]
    p = _each(lambda a, b: a + _bdot(a, b), p, l2)
    l4 = _each(_bdot, l2, l2)
    p = _each(lambda a, b: a + _bdot(a, b), p, l4)
    l8 = _each(_bdot, l4, l4)
    dinv = _each(lambda a, b: a + _bdot(a, b), p, l8)
    nm = _each(_bdot, dinv, e)
    n2 = _each(_bdot, nm, nm)
    a1 = _each(lambda a, b: a + _bdot(b, a), dinv, nm)
    return _each(lambda a, b: a + _bdot(b, a), a1, n2)


def _scan_compute(r_ref, k_ref, v_ref, lw_ref, kk_ref, a_ref, g_ref, rk_ref, lnw_ref, lnb_ref,
                  y_ref, s_ref, *, tt, npair):
    c = CHUNK
    nch = tt // c

    @pl.when(pl.program_id(1) == 0)
    def _():
        s_ref[...] = jnp.zeros_like(s_ref)

    row = lax.broadcasted_iota(jnp.int32, (c, LANES), 0)
    lane = lax.broadcasted_iota(jnp.int32, (c, LANES), 1)
    is_a = lane < HEAD_DIM
    s_idx = lane % HEAD_DIM
    strict = s_idx < row
    incl = s_idx <= row
    strict_a = is_a & strict
    strict_b = (~is_a) & strict
    r2 = lax.broadcasted_iota(jnp.int32, (2 * c, LANES), 0)
    l2 = lax.broadcasted_iota(jnp.int32, (2 * c, LANES), 1)
    d16 = (r2 // 16) == (l2 // 16)
    same_head = (r2 // HEAD_DIM) == (l2 // HEAD_DIM)
    tri_incl = (lax.broadcasted_iota(jnp.int32, (c, c), 1)
                <= lax.broadcasted_iota(jnp.int32, (c, c), 0)).astype(BF16)
    gsum = _head_group_matrix(1.0)

    def head_stack(x):
        return jnp.concatenate([jnp.where(is_a, x, 0.0), jnp.where(is_a, 0.0, x)], axis=0)

    def head_stack_swapped(x):
        return jnp.concatenate([jnp.where(is_a, 0.0, x), jnp.where(is_a, x, 0.0)], axis=0)

    elems = [(ci, p) for ci in range(nch) for p in range(npair)]

    def tiles(ref):
        return [ref[ci * c:(ci + 1) * c, p * LANES:(p + 1) * LANES].astype(F32)
                for ci, p in elems]

    kkr = tiles(kk_ref)
    lw = tiles(lw_ref)
    ss = _stacked_dot([x * x for x in kkr], gsum)
    c_in = [_split2_dot(tri_incl, x) for x in lw]
    kk = _each(lambda x, s: x * lax.rsqrt(jnp.maximum(s, 1e-24)), kkr, ss)
    b = _each(lambda x, a: x * a, kk, tiles(a_ref))
    e_tot = [jnp.exp(x[c - 1:c, :]) for x in c_in]
    e_neg = [jnp.exp(-x) for x in c_in]
    at = _each(lambda x, ci_, l: -x * jnp.exp(ci_ - l), kk, c_in, lw)
    rt = _each(lambda x, ci_: x * jnp.exp(ci_), tiles(r_ref), c_in)
    bt = _each(lambda x, en: x * en, b, e_neg)
    kt = _each(lambda x, en: x * en, tiles(k_ref), e_neg)
    v = tiles(v_ref)
    ar = _each(lambda x, y: jnp.concatenate([x, y], axis=0).astype(BF16), at, rt)
    bkt = _each(lambda x, y: jnp.concatenate([x, y], axis=0).astype(BF16), bt, kt)
    lhs = [jnp.concatenate([jnp.where(l2 < HEAD_DIM, x, jnp.zeros_like(x)),
                            jnp.where(l2 < HEAD_DIM, jnp.zeros_like(x), x)], axis=0) for x in ar]
    g_ab = _each(lambda x, y: lax.dot_general(x, y, NT_DIMS, preferred_element_type=F32),
                 lhs, bkt)
    g_a = [x[0:2 * c] for x in g_ab]
    g_b = [pltpu.roll(x[2 * c:4 * c], HEAD_DIM, axis=1) for x in g_ab]
    lps = _each(lambda ga, gb: jnp.concatenate([jnp.where(strict_a, ga[0:c], 0.0),
                                                jnp.where(strict_b, gb[0:c], 0.0)], axis=0),
                g_a, g_b)
    lmk = _each(lambda ga, gb: jnp.concatenate(
        [jnp.where(strict, jnp.where(is_a, gb[0:c], ga[0:c]), 0.0),
         jnp.where(incl, jnp.where(is_a, gb[c:2 * c], ga[c:2 * c]), 0.0)], axis=0), g_a, g_b)
    mrb = _each(lambda ga, gb: jnp.where(incl, jnp.where(is_a, ga[c:2 * c], gb[c:2 * c]),
                                         0.0).astype(BF16), g_a, g_b)
    lv = _each(lambda m, x: _bdot(m, head_stack_swapped(x)), lmk, v)
    bk_h = _each(lambda x, y, et: jnp.concatenate([x * et, y * et], axis=0).astype(BF16),
                 bt, kt, e_tot)
    tps = [x.astype(BF16) for x in _unit_lower_inverses(lps, d16)]

    s = [s_ref[p] for p in range(npair)]
    ys = []
    for ci in range(nch):
        idx = [ci * npair + p for p in range(npair)]
        ah = [lax.dot_general(ar[i], s[p].astype(BF16), NT_DIMS, preferred_element_type=F32)
              for p, i in enumerate(idx)]
        x = [ah[p][0:c] + lv[i][0:c] for p, i in enumerate(idx)]
        tu = [jnp.dot(tps[i], head_stack(x[p]).astype(BF16), preferred_element_type=F32)
              for p, i in enumerate(idx)]
        u = [t[0:c] + t[c:2 * c] for t in tu]
        ys.append([ah[p][c:2 * c] + lv[i][c:2 * c]
                   + jnp.dot(mrb[i], head_stack(u[p]).astype(BF16), preferred_element_type=F32)
                   for p, i in enumerate(idx)])
        uv_t = [jnp.concatenate([u[p], v[i]], axis=0).T.astype(BF16)
                for p, i in enumerate(idx)]
        s = [s[p] * e_tot[i]
             + jnp.where(same_head, jnp.dot(uv_t[p], bk_h[i], preferred_element_type=F32), 0.0)
             for p, i in enumerate(idx)]
    for p in range(npair):
        s_ref[p] = s[p]

    gavg = _head_group_matrix(1.0 / HEAD_DIM)
    cols = [slice(p * LANES, (p + 1) * LANES) for p in range(npair)]
    y = [jnp.concatenate([ys[ci][p] for ci in range(nch)], axis=0) for p in range(npair)]
    rk = [r_ref[:, cs].astype(F32) * k_ref[:, cs].astype(F32) * rk_ref[:, cs] for cs in cols]
    mean = _stacked_dot(y, gavg)
    rk_sum = _stacked_dot(rk, gsum)
    yc = _each(lambda a, b: a - b, y, mean)
    var = _stacked_dot([x * x for x in yc], gavg)
    for p, cs in enumerate(cols):
        yn = yc[p] * lax.rsqrt(var[p] + LNX_EPS) * lnw_ref[:, cs] + lnb_ref[:, cs]
        bonus = rk_sum[p] * v_ref[:, cs].astype(F32)
        y_ref[:, cs] = ((yn + bonus) * g_ref[:, cs].astype(F32)).astype(y_ref.dtype)


def _rwkv_mix_kernel(*refs, tt, npair, has_vres):
    n_prep = 16 if has_vres else 13
    prep_in = refs[0:n_prep]
    rk_ref, lnw_ref, lnb_ref = refs[n_prep:n_prep + 3]
    n_out = 1 if has_vres else 2
    outs = refs[n_prep + 3:n_prep + 3 + n_out]
    s_ref, ca_ref, cc_ref, cb_ref = refs[n_prep + 3 + n_out:n_prep + 7 + n_out]
    work = refs[n_prep + 7 + n_out:]
    _prep_compute(prep_in, work, (ca_ref, cc_ref, cb_ref), has_vres=has_vres, tt=tt)
    if not has_vres:
        outs[1][...] = work[2][...].astype(outs[1].dtype)
    _scan_compute(*work, rk_ref, lnw_ref, lnb_ref, outs[0], s_ref, tt=tt, npair=npair)


def _rwkv_mix(proj, vfirst, lp, r_k, lnx_w, lnx_b, *, bsz, seq, tt):
    n = bsz * seq
    nt = seq // tt
    has_vres = vfirst is not None
    d = D_RWKV
    npair = d // LANES
    row = lambda b, t: (b * nt + t, 0)
    const = lambda b, t: (0, 0)
    rowblk = pl.BlockSpec((tt, d), row)
    par = pl.BlockSpec((1, d), const)
    in_specs = [
        pl.BlockSpec((tt, 3 * d), lambda b, t: (b * nt + t, SEG_A // (3 * d))),
        pl.BlockSpec((tt, 256), lambda b, t: (b * nt + t, SEG_C // 256)),
        pl.BlockSpec((tt, 128), lambda b, t: (b * nt + t, SEG_B // 128)),
        pl.BlockSpec((1, 3 * d), const),
        pl.BlockSpec((1, 256), const),
        pl.BlockSpec((1, 128), const),
        par,
        pl.BlockSpec((128, d), const),
        par,
        pl.BlockSpec((128, d), const),
        pl.BlockSpec((256, d), const),
        par,
        par,
    ]
    args = [proj, proj, proj, lp["mu_a"], lp["mu_c"], lp["mu_b"], lp["w0"], lp["wup"], lp["a0"],
            lp["aup"], lp["gup"], lp["k_k"], lp["k_a"]]
    if has_vres:
        in_specs += [rowblk, par, pl.BlockSpec((256, d), const)]
        args += [vfirst, lp["v0"], lp["vup"]]
    in_specs += [par] * 3
    args += [r_k.reshape(1, d), lnx_w.reshape(1, d), lnx_b.reshape(1, d)]
    n_out = 1 if has_vres else 2
    outs = pl.pallas_call(
        functools.partial(_rwkv_mix_kernel, tt=tt, npair=npair, has_vres=has_vres),
        grid=(bsz, nt),
        in_specs=in_specs,
        out_specs=[rowblk] * n_out,
        out_shape=[jax.ShapeDtypeStruct((n, d), BF16)] * n_out,
        scratch_shapes=[pltpu.VMEM((npair, LANES, LANES), F32),
                        pltpu.VMEM((8, 3 * d), F32), pltpu.VMEM((8, 256), F32),
                        pltpu.VMEM((8, 128), F32)] + [pltpu.VMEM((tt, d), F32)] * 7,
        compiler_params=_cparams(("parallel", "arbitrary")),
        name="rwkv_mix",
    )(*args)
    return (outs[0], None) if has_vres else (outs[0], outs[1])


def _sb_kernel(q_ref, k_ref, v_ref, g_ref, o_ref, vab_ref, *, blk, nblk, ppg, scale):
    i = pl.program_id(2)
    lane = lax.broadcasted_iota(jnp.int32, (blk, LANES), 1)
    is_a = lane < HEAD_DIM
    nh = 2 * ppg
    cols = [slice(p * LANES, (p + 1) * LANES) for p in range(ppg)]

    @pl.when(i == 0)
    def _():
        for p in range(ppg):
            for j in range(nblk):
                vb = v_ref[j * blk:(j + 1) * blk, cols[p]]
                vab_ref[p, j, 0:blk, :] = jnp.where(is_a, vb, jnp.zeros_like(vb))
                vab_ref[p, j, blk:2 * blk, :] = jnp.where(is_a, jnp.zeros_like(vb), vb)

    tq = lax.broadcasted_iota(jnp.int32, (blk, blk), 0)
    sk = lax.broadcasted_iota(jnp.int32, (blk, blk), 1)
    causal = sk < tq
    suffix = (tq > sk).astype(BF16)

    q_heads = []
    for p in range(ppg):
        q = q_ref[:, cols[p]].astype(F32) * (scale * LOG2E)
        q_heads += [jnp.where(is_a, q, 0.0).astype(BF16), jnp.where(is_a, 0.0, q).astype(BF16)]

    def block_steps(js, state, diag):
        carries = list(state[0:nh])
        accs = list(state[nh:nh + ppg])
        kbs = [[k_ref[pl.ds(pl.multiple_of(j * blk, blk), blk), cols[p]] for p in range(ppg)]
               for j in js]
        z2 = [[lax.dot_general(q_heads[h], kb[h // 2], NT_DIMS, preferred_element_type=F32)
               for h in range(nh)] for kb in kbs]
        if diag:
            z2[0] = [jnp.where(causal, z, -jnp.inf) for z in z2[0]]
        sp = [[_softplus2(z) for z in zs] for zs in z2]
        spb = [[s.astype(BF16) for s in ss] for ss in sp]
        lsig = [_each(lambda z, s: z - s, zs, ss) for zs, ss in zip(z2, sp)]
        if diag:
            between = [[jnp.dot(s, suffix, preferred_element_type=F32) for s in ss] for ss in spb]
        else:
            bt_all = jnp.dot(jnp.concatenate([s for ss in spb for s in ss], axis=0), suffix,
                             preferred_element_type=F32)
            between = [[bt_all[(b * nh + h) * blk:(b * nh + h + 1) * blk] for h in range(nh)]
                       for b in range(len(js))]
        for b, j in enumerate(js):
            atts = []
            for h in range(nh):
                atts.append(jnp.exp2((lsig[b][h] - between[b][h] - carries[h]).astype(BF16)))
                carries[h] = carries[h] + between[b][h][:, 0:1] + sp[b][h][:, 0:1]
            for p in range(ppg):
                accs[p] = accs[p] + jnp.dot(jnp.concatenate(atts[2 * p:2 * p + 2], axis=1),
                                            vab_ref[p, j], preferred_element_type=F32)
        return tuple(carries) + tuple(accs)

    state = (jnp.zeros((blk, 1), F32),) * nh + (jnp.zeros((blk, LANES), F32),) * ppg
    first = jnp.where(i == 0, 0, 2 - i % 2)
    state = lax.switch(first, [lambda s: block_steps([i], s, True),
                               lambda s: block_steps([i, i - 1], s, True),
                               lambda s: block_steps([i, i - 1, i - 2], s, True)], state)

    def body(m, s):
        j = i - 1 - first - 2 * m
        return block_steps([j, j - 1], s, False)

    state = lax.fori_loop(0, (i - first) // 2, body, state)

    gavg = _head_group_matrix(1.0 / HEAD_DIM)
    accs = [state[nh + p] for p in range(ppg)]
    ms = [_bdot(acc * acc, gavg) for acc in accs]
    for p in range(ppg):
        o_ref[:, cols[p]] = (accs[p] * lax.rsqrt(ms[p] + NORM_EPS)
                             * g_ref[:, cols[p]]).astype(o_ref.dtype)


def _sb_attention(qkv, gain, *, bsz, seq, blk, ppg):
    n = bsz * seq
    nq = seq // blk
    ngrp = D_SB // (LANES * ppg)
    w = LANES * ppg
    return pl.pallas_call(
        functools.partial(_sb_kernel, blk=blk, nblk=nq, ppg=ppg, scale=1.0 / (HEAD_DIM ** 0.5)),
        grid=(bsz, ngrp, nq),
        in_specs=[
            pl.BlockSpec((blk, w), lambda b, h, i: (b * nq + i, h)),
            pl.BlockSpec((seq, w), lambda b, h, i: (b, ngrp + h)),
            pl.BlockSpec((seq, w), lambda b, h, i: (b, 2 * ngrp + h)),
            pl.BlockSpec((1, w), lambda b, h, i: (0, h)),
        ],
        out_specs=pl.BlockSpec((blk, w), lambda b, h, i: (b * nq + i, h)),
        out_shape=jax.ShapeDtypeStruct((n, D_SB), BF16),
        scratch_shapes=[pltpu.VMEM((ppg, nq, 2 * blk, LANES), BF16)],
        compiler_params=_cparams(("parallel", "parallel", "arbitrary")),
        name="sb_attention",
    )(qkv, qkv, qkv, gain.reshape(1, D_SB))


C_DW = 3 * D_RWKV
C_DG = C_DW + W_LORA + A_LORA
C_SB = C_DG + G_LORA


def _projection_weights(w_in, w_in_vres):
    depth, d_model, _ = w_in.shape
    zeros = lambda lead, m: jnp.zeros((lead, d_model, m), F32)
    vres = jnp.concatenate([zeros(1, V_LORA), w_in_vres], axis=0)
    w_rwkv = jnp.concatenate([
        w_in[:, :, 0:C_DW],
        w_in[:, :, C_DG:C_SB], vres, zeros(depth, 256 - G_LORA - V_LORA),
        w_in[:, :, C_DW:C_DG],
        zeros(depth, P_RWKV - SEG_B - 128),
    ], axis=2).astype(BF16)
    return w_rwkv, w_in[:, :, C_SB:].astype(BF16)


def _layer_params(l, mu, mu_vres, w0, w_up, a0, a_up, g_up, v0, v_up, k_k, k_a):
    d = D_RWKV
    c_dw, c_dg, c_sb = C_DW, C_DG, C_SB
    zrow = lambda m: jnp.zeros((m,), F32)
    vres_mu = mu_vres[l - 1] if l > 0 else zrow(V_LORA)
    mul = mu[l]
    lp = {
        "mu_a": mul[0:c_dw].reshape(1, -1),
        "mu_c": jnp.concatenate([mul[c_dg:c_sb], vres_mu,
                                 zrow(256 - G_LORA - V_LORA)]).reshape(1, -1),
        "mu_b": mul[c_dw:c_dg].reshape(1, -1),
        "w0": w0[l].reshape(1, d),
        "a0": a0[l].reshape(1, d),
        "k_k": k_k[l].reshape(1, d),
        "k_a": k_a[l].reshape(1, d),
        "wup": jnp.concatenate([w_up[l], jnp.zeros((A_LORA, d), F32)], axis=0).astype(BF16),
        "aup": jnp.concatenate([jnp.zeros((W_LORA, d), F32), a_up[l]], axis=0).astype(BF16),
        "gup": jnp.concatenate([g_up[l], jnp.zeros((256 - G_LORA, d), F32)],
                               axis=0).astype(BF16),
    }
    if l > 0:
        lp["v0"] = v0[l - 1].reshape(1, d)
        lp["vup"] = jnp.concatenate([jnp.zeros((G_LORA, d), F32), v_up[l - 1],
                                     jnp.zeros((256 - G_LORA - V_LORA, d), F32)],
                                    axis=0).astype(BF16)
    return lp


def kernel(x, pre_mix_g, post_mix_g, pre_mlp_g, post_mlp_g, w_in, w_in_vres, mu, mu_vres, w0, w_up,
           a0, a_up, g_up, v0, v_up, k_k, k_a, r_k, lnx_w, lnx_b, sb_out_g, w_out, w_ff_up,
           w_ff_down):
    bsz, seq, d_model = x.shape
    depth = w_in.shape[0]
    n = bsz * seq
    assert w_in.shape[2] == 3 * D_RWKV + W_LORA + A_LORA + G_LORA + 3 * D_SB
    tm_mm = min(1024, n)
    tm_res = min(512, n)
    xf = x.reshape(n, d_model)
    h = _norm_cast(xf, pre_mix_g[0], tm=tm_res)
    w_out_b, w_up_b, w_down_b = (w.astype(BF16) for w in (w_out, w_ff_up, w_ff_down))
    w_rwkv_b, w_sb_b = _projection_weights(w_in, w_in_vres)
    vfirst = None
    for l in range(depth):
        lp = _layer_params(l, mu, mu_vres, w0, w_up, a0, a_up, g_up, v0, v_up, k_k, k_a)
        proj = _matmul(h, w_rwkv_b, l, relu2=False, tm=tm_mm, tn=P_RWKV // 2, name="proj_rwkv")
        qkv = _matmul(h, w_sb_b, l, relu2=False, tm=tm_mm, tn=1536, name="proj_sb")
        y_r, v_l = _rwkv_mix(proj, vfirst, lp, r_k[l], lnx_w[l], lnx_b[l],
                             bsz=bsz, seq=seq, tt=min(256, seq))
        if l == 0:
            vfirst = v_l
        y_s = _sb_attention(qkv, sb_out_g[l], bsz=bsz, seq=seq, blk=min(256, seq), ppg=4)
        xf, h = _mix_out(y_r, y_s, w_out_b, l, xf, post_mix_g[l], pre_mlp_g[l], tm=tm_res)
        ff = _matmul(h, w_up_b, l, relu2=True, tm=tm_mm, tn=2048, name="ffn_up")
        g_next = pre_mix_g[l + 1] if l + 1 < depth else None
        xf, h = _ffn_down(ff, w_down_b, l, xf, post_mlp_g[l], g_next, tm=tm_res, tk=2048)
    return xf.reshape(bsz, seq, d_model)
```

```python
import functools

import jax
import jax.numpy as jnp
from jax import lax
from jax.experimental import pallas as pl
from jax.experimental.pallas import tpu as pltpu

F32 = jnp.float32
BF16 = jnp.bfloat16

HEAD_DIM = 64
LANES = 128
NORM_EPS = 1e-6
LNX_EPS = 64e-5
CHUNK = 64
VMEM_LIMIT = 48 * 1024 * 1024
LOG2E = 1.4426950408889634
DECAY_SCALE = 0.6065306597126334

D_RWKV = 1024
D_SB = 1024
W_LORA, A_LORA, V_LORA, G_LORA = 64, 64, 32, 160
SEG_A = 0
SEG_C = 3 * D_RWKV
SEG_B = SEG_C + 256
P_RWKV = 3584

NT_DIMS = (((1,), (1,)), ((), ()))


def _cparams(sem):
    return pltpu.CompilerParams(dimension_semantics=sem, vmem_limit_bytes=VMEM_LIMIT)


def _bdot(a, b):
    return jnp.dot(a.astype(BF16), b.astype(BF16), preferred_element_type=F32)


def _split2_dot(a_exact, x):
    hi = x.astype(BF16)
    lo = (x - hi.astype(F32)).astype(BF16)
    return (jnp.dot(a_exact, hi, preferred_element_type=F32)
            + jnp.dot(a_exact, lo, preferred_element_type=F32))


def _head_group_matrix(scale):
    i = lax.broadcasted_iota(jnp.int32, (LANES, LANES), 0) // HEAD_DIM
    j = lax.broadcasted_iota(jnp.int32, (LANES, LANES), 1) // HEAD_DIM
    return jnp.where(i == j, scale, 0.0).astype(BF16)


def _sigmoid(x):
    return 0.5 * jnp.tanh(0.5 * x) + 0.5


def _neg_abs(x):
    bits = lax.bitcast_convert_type(x, jnp.uint32) | jnp.uint32(0x80000000)
    return lax.bitcast_convert_type(bits, F32)


def _softplus2(z2):
    return jnp.maximum(z2, 0.0) + jnp.log2(1.0 + jnp.exp2(_neg_abs(z2)))


def _stacked_dot(xs, w):
    rows = xs[0].shape[0]
    out = _bdot(jnp.concatenate(xs, axis=0), w)
    return [out[i * rows:(i + 1) * rows] for i in range(len(xs))]


def _each(fn, *lists):
    return [fn(*xs) for xs in zip(*lists)]


def _rms_scaled(x, g):
    ms = jnp.mean(x * x, axis=-1, keepdims=True)
    return x * lax.rsqrt(ms + NORM_EPS) * g


def _norm_cast_kernel(x_ref, g_ref, h_ref):
    h_ref[...] = _rms_scaled(x_ref[...], g_ref[...]).astype(h_ref.dtype)


def _norm_cast(x, g, *, tm):
    n, d = x.shape
    return pl.pallas_call(
        _norm_cast_kernel,
        grid=(n // tm,),
        in_specs=[pl.BlockSpec((tm, d), lambda i: (i, 0)), pl.BlockSpec((1, d), lambda i: (0, 0))],
        out_specs=pl.BlockSpec((tm, d), lambda i: (i, 0)),
        out_shape=jax.ShapeDtypeStruct((n, d), BF16),
        compiler_params=_cparams(("parallel",)),
        name="norm_cast",
    )(x, g.reshape(1, d))


def _matmul_kernel(a_ref, w_ref, o_ref, *, relu2):
    y = jnp.dot(a_ref[...], w_ref[...], preferred_element_type=F32)
    if relu2:
        y = jnp.square(jnp.maximum(y, 0.0))
    o_ref[...] = y.astype(o_ref.dtype)


def _matmul(a, w, layer, *, relu2, tm, tn, name):
    n, kdim = a.shape
    p = w.shape[2]
    return pl.pallas_call(
        functools.partial(_matmul_kernel, relu2=relu2),
        grid=(n // tm, p // tn),
        in_specs=[
            pl.BlockSpec((tm, kdim), lambda i, j: (i, 0)),
            pl.BlockSpec((None, kdim, tn), lambda i, j: (layer, 0, j)),
        ],
        out_specs=pl.BlockSpec((tm, tn), lambda i, j: (i, j)),
        out_shape=jax.ShapeDtypeStruct((n, p), BF16),
        compiler_params=_cparams(("parallel", "arbitrary")),
        name=name,
    )(a, w)


def _residual_epilogue(acc, x_ref, g_ref, gn_ref, o_ref, h_ref):
    xn = x_ref[...] + _rms_scaled(acc, g_ref[...])
    o_ref[...] = xn
    if h_ref is not None:
        h_ref[...] = _rms_scaled(xn, gn_ref[...]).astype(h_ref.dtype)


def _mix_out_kernel(a1_ref, a2_ref, w1_ref, w2_ref, x_ref, g_ref, gn_ref, o_ref, h_ref):
    acc = jnp.dot(a1_ref[...], w1_ref[...], preferred_element_type=F32)
    acc = acc + jnp.dot(a2_ref[...], w2_ref[...], preferred_element_type=F32)
    _residual_epilogue(acc, x_ref, g_ref, gn_ref, o_ref, h_ref)


def _mix_out(a1, a2, w, layer, x, g, g_next, *, tm):
    n, d = x.shape
    k1 = a1.shape[1]
    k2 = a2.shape[1]
    assert k1 == k2 and w.shape[1] == k1 + k2
    return pl.pallas_call(
        _mix_out_kernel,
        grid=(n // tm,),
        in_specs=[
            pl.BlockSpec((tm, k1), lambda i: (i, 0)),
            pl.BlockSpec((tm, k2), lambda i: (i, 0)),
            pl.BlockSpec((None, k1, d), lambda i: (layer, 0, 0)),
            pl.BlockSpec((None, k2, d), lambda i: (layer, 1, 0)),
            pl.BlockSpec((tm, d), lambda i: (i, 0)),
            pl.BlockSpec((1, d), lambda i: (0, 0)),
            pl.BlockSpec((1, d), lambda i: (0, 0)),
        ],
        out_specs=[pl.BlockSpec((tm, d), lambda i: (i, 0))] * 2,
        out_shape=[jax.ShapeDtypeStruct((n, d), F32), jax.ShapeDtypeStruct((n, d), BF16)],
        compiler_params=_cparams(("parallel",)),
        name="mix_out",
    )(a1, a2, w, w, x, g.reshape(1, d), g_next.reshape(1, d))


def _ffn_down_kernel(*refs, nk, emit_h):
    if emit_h:
        a_ref, w_ref, x_ref, g_ref, gn_ref, o_ref, h_ref = refs
    else:
        a_ref, w_ref, x_ref, g_ref, o_ref = refs
        gn_ref = h_ref = None
    k = pl.program_id(1)
    part = jnp.dot(a_ref[...], w_ref[...], preferred_element_type=F32)

    @pl.when(k == 0)
    def _():
        o_ref[...] = part

    @pl.when(k > 0)
    def _():
        o_ref[...] += part

    @pl.when(k == nk - 1)
    def _():
        _residual_epilogue(o_ref[...], x_ref, g_ref, gn_ref, o_ref, h_ref)


def _ffn_down(a, w, layer, x, g, g_next, *, tm, tk):
    n, d = x.shape
    kdim = a.shape[1]
    nk = kdim // tk
    emit_h = g_next is not None
    row = pl.BlockSpec((tm, d), lambda i, k: (i, 0))
    gain = pl.BlockSpec((1, d), lambda i, k: (0, 0))
    in_specs = [pl.BlockSpec((tm, tk), lambda i, k: (i, k)),
                pl.BlockSpec((None, tk, d), lambda i, k: (layer, k, 0)), row, gain]
    args = [a, w, x, g.reshape(1, d)]
    out_specs = [row]
    out_shape = [jax.ShapeDtypeStruct((n, d), F32)]
    if emit_h:
        in_specs.append(gain)
        args.append(g_next.reshape(1, d))
        out_specs.append(row)
        out_shape.append(jax.ShapeDtypeStruct((n, d), BF16))
    outs = pl.pallas_call(
        functools.partial(_ffn_down_kernel, nk=nk, emit_h=emit_h),
        grid=(n // tm, nk),
        in_specs=in_specs,
        out_specs=out_specs,
        out_shape=out_shape,
        compiler_params=_cparams(("parallel", "arbitrary")),
        name="ffn_down",
    )(*args)
    return (outs[0], outs[1]) if emit_h else (outs[0], None)


def _prep_compute(in_refs, out_refs, carry_refs, *, has_vres, tt):
    (pa_ref, pc_ref, pb_ref, mua_ref, muc_ref, mub_ref, w0_ref, wup_ref, a0_ref, aup_ref,
     gup_ref, kk_ref, ka_ref) = in_refs[0:13]
    if has_vres:
        vf_ref, v0_ref, vup_ref = in_refs[13:16]
    r_out, k_out, v_out, lw_out, kk_out, a_out, g_out = out_refs
    ca_ref, cc_ref, cb_ref = carry_refs

    @pl.when(pl.program_id(1) == 0)
    def _():
        ca_ref[...] = jnp.zeros_like(ca_ref)
        cc_ref[...] = jnp.zeros_like(cc_ref)
        cb_ref[...] = jnp.zeros_like(cb_ref)

    down = (lax.broadcasted_iota(jnp.int32, (tt, tt), 1) + 1
            == lax.broadcasted_iota(jnp.int32, (tt, tt), 0)).astype(BF16)

    def shifted(pb16, prev_row, mu):
        p = pb16.astype(F32)
        prev = jnp.dot(down, pb16, preferred_element_type=F32)
        row = lax.broadcasted_iota(jnp.int32, p.shape, 0)
        prev = jnp.where(row == 0, prev_row, prev)
        return p + (prev - p) * mu

    xb = shifted(pb_ref[...], cb_ref[7:8, :], mub_ref[...])
    xc = shifted(pc_ref[...], cc_ref[7:8, :], muc_ref[...])

    tw = jnp.tanh(w0_ref[...] + _bdot(jnp.tanh(xb), wup_ref[...]))
    lw_out[...] = tw * (-0.5 * DECAY_SCALE) - 0.5 * DECAY_SCALE
    a = 0.5 * jnp.tanh(a0_ref[...] + _bdot(xb, aup_ref[...])) + 0.5
    a_out[...] = a.astype(a_out.dtype)
    g_out[...] = _bdot(_sigmoid(xc), gup_ref[...]).astype(g_out.dtype)

    d = D_RWKV
    r_out[...] = shifted(pa_ref[:, 0:d], ca_ref[7:8, 0:d], mua_ref[:, 0:d]).astype(r_out.dtype)
    k = shifted(pa_ref[:, d:2 * d], ca_ref[7:8, d:2 * d], mua_ref[:, d:2 * d])
    kk_out[...] = (k * kk_ref[...]).astype(kk_out.dtype)
    k_out[...] = (k * (1.0 + (a - 1.0) * ka_ref[...])).astype(k_out.dtype)
    v = shifted(pa_ref[:, 2 * d:3 * d], ca_ref[7:8, 2 * d:3 * d], mua_ref[:, 2 * d:3 * d])
    if has_vres:
        mix = 0.5 * jnp.tanh(v0_ref[...] + _bdot(xc, vup_ref[...])) + 0.5
        v = v + (vf_ref[...].astype(F32) - v) * mix
    v_out[...] = v.astype(v_out.dtype)

    ca_ref[...] = pa_ref[tt - 8:tt, :].astype(F32)
    cc_ref[...] = pc_ref[tt - 8:tt, :].astype(F32)
    cb_ref[...] = pb_ref[tt - 8:tt, :].astype(F32)


def _unit_lower_inverses(lps, d16):
    n = lps[0].shape[0]
    eye = (lax.broadcasted_iota(jnp.int32, (n, n), 0)
           == lax.broadcasted_iota(jnp.int32, (n, n), 1)).astype(F32)
    ld = [jnp.where(d16, lp, 0.0) for lp in lps]
    e = _each(lambda a, b: a - b, lps, ld)
    l2 = _each(_bdot, ld, ld)
    p = [eye + x for x in ld]
    p = _each(lambda a, b: a + _bdot(a, b), p, l2)
    l4 = _each(_bdot, l2, l2)
    p = _each(lambda a, b: a + _bdot(a, b), p, l4)
    l8 = _each(_bdot, l4, l4)
    dinv = _each(lambda a, b: a + _bdot(a, b), p, l8)
    nm = _each(_bdot, dinv, e)
    n2 = _each(_bdot, nm, nm)
    a1 = _each(lambda a, b: a + _bdot(b, a), dinv, nm)
    return _each(lambda a, b: a + _bdot(b, a), a1, n2)


def _scan_compute(r_ref, k_ref, v_ref, lw_ref, kk_ref, a_ref, g_ref, rk_ref, lnw_ref, lnb_ref,
                  y_ref, s_ref, *, tt, npair):
    c = CHUNK
    nch = tt // c

    @pl.when(pl.program_id(1) == 0)
    def _():
        s_ref[...] = jnp.zeros_like(s_ref)

    row = lax.broadcasted_iota(jnp.int32, (c, LANES), 0)
    lane = lax.broadcasted_iota(jnp.int32, (c, LANES), 1)
    is_a = lane < HEAD_DIM
    s_idx = lane % HEAD_DIM
    strict = s_idx < row
    incl = s_idx <= row
    strict_a = is_a & strict
    strict_b = (~is_a) & strict
    r2 = lax.broadcasted_iota(jnp.int32, (2 * c, LANES), 0)
    l2 = lax.broadcasted_iota(jnp.int32, (2 * c, LANES), 1)
    d16 = (r2 // 16) == (l2 // 16)
    same_head = (r2 // HEAD_DIM) == (l2 // HEAD_DIM)
    tri_incl = (lax.broadcasted_iota(jnp.int32, (c, c), 1)
                <= lax.broadcasted_iota(jnp.int32, (c, c), 0)).astype(BF16)
    gsum = _head_group_matrix(1.0)

    def head_stack(x):
        return jnp.concatenate([jnp.where(is_a, x, 0.0), jnp.where(is_a, 0.0, x)], axis=0)

    def head_stack_swapped(x):
        return jnp.concatenate([jnp.where(is_a, 0.0, x), jnp.where(is_a, x, 0.0)], axis=0)

    elems = [(ci, p) for ci in range(nch) for p in range(npair)]

    def tiles(ref):
        return [ref[ci * c:(ci + 1) * c, p * LANES:(p + 1) * LANES].astype(F32)
                for ci, p in elems]

    kkr = tiles(kk_ref)
    lw = tiles(lw_ref)
    ss = _stacked_dot([x * x for x in kkr], gsum)
    c_in = [_split2_dot(tri_incl, x) for x in lw]
    kk = _each(lambda x, s: x * lax.rsqrt(jnp.maximum(s, 1e-24)), kkr, ss)
    b = _each(lambda x, a: x * a, kk, tiles(a_ref))
    e_tot = [jnp.exp(x[c - 1:c, :]) for x in c_in]
    e_neg = [jnp.exp(-x) for x in c_in]
    at = _each(lambda x, ci_, l: -x * jnp.exp(ci_ - l), kk, c_in, lw)
    rt = _each(lambda x, ci_: x * jnp.exp(ci_), tiles(r_ref), c_in)
    bt = _each(lambda x, en: x * en, b, e_neg)
    kt = _each(lambda x, en: x * en, tiles(k_ref), e_neg)
    v = tiles(v_ref)
    ar = _each(lambda x, y: jnp.concatenate([x, y], axis=0).astype(BF16), at, rt)
    bkt = _each(lambda x, y: jnp.concatenate([x, y], axis=0).astype(BF16), bt, kt)
    lhs = [jnp.concatenate([jnp.where(l2 < HEAD_DIM, x, jnp.zeros_like(x)),
                            jnp.where(l2 < HEAD_DIM, jnp.zeros_like(x), x)], axis=0) for x in ar]
    g_ab = _each(lambda x, y: lax.dot_general(x, y, NT_DIMS, preferred_element_type=F32),
                 lhs, bkt)
    g_a = [x[0:2 * c] for x in g_ab]
    g_b = [pltpu.roll(x[2 * c:4 * c], HEAD_DIM, axis=1) for x in g_ab]
    lps = _each(lambda ga, gb: jnp.concatenate([jnp.where(strict_a, ga[0:c], 0.0),
                                                jnp.where(strict_b, gb[0:c], 0.0)], axis=0),
                g_a, g_b)
    lmk = _each(lambda ga, gb: jnp.concatenate(
        [jnp.where(strict, jnp.where(is_a, gb[0:c], ga[0:c]), 0.0),
         jnp.where(incl, jnp.where(is_a, gb[c:2 * c], ga[c:2 * c]), 0.0)], axis=0), g_a, g_b)
    mrb = _each(lambda ga, gb: jnp.where(incl, jnp.where(is_a, ga[c:2 * c], gb[c:2 * c]),
                                         0.0).astype(BF16), g_a, g_b)
    lv = _each(lambda m, x: _bdot(m, head_stack_swapped(x)), lmk, v)
    bk_h = _each(lambda x, y, et: jnp.concatenate([x * et, y * et], axis=0).astype(BF16),
                 bt, kt, e_tot)
    n_first = max(nch // 2, 1) * npair
    tps = [x.astype(BF16) for x in _unit_lower_inverses(lps[:n_first], d16)]

    s = [s_ref[p] for p in range(npair)]
    ys = []
    for ci in range(nch):
        if ci * npair == n_first:
            tps += [x.astype(BF16) for x in _unit_lower_inverses(lps[n_first:], d16)]
        idx = [ci * npair + p for p in range(npair)]
        ah = [lax.dot_general(ar[i], s[p].astype(BF16), NT_DIMS, preferred_element_type=F32)
              for p, i in enumerate(idx)]
        x = [ah[p][0:c] + lv[i][0:c] for p, i in enumerate(idx)]
        tu = [jnp.dot(tps[i], head_stack(x[p]).astype(BF16), preferred_element_type=F32)
              for p, i in enumerate(idx)]
        u = [t[0:c] + t[c:2 * c] for t in tu]
        ys.append([ah[p][c:2 * c] + lv[i][c:2 * c]
                   + jnp.dot(mrb[i], head_stack(u[p]).astype(BF16), preferred_element_type=F32)
                   for p, i in enumerate(idx)])
        uv_t = [jnp.concatenate([u[p], v[i]], axis=0).T.astype(BF16)
                for p, i in enumerate(idx)]
        s = [s[p] * e_tot[i]
             + jnp.where(same_head, jnp.dot(uv_t[p], bk_h[i], preferred_element_type=F32), 0.0)
             for p, i in enumerate(idx)]
    for p in range(npair):
        s_ref[p] = s[p]

    gavg = _head_group_matrix(1.0 / HEAD_DIM)
    cols = [slice(p * LANES, (p + 1) * LANES) for p in range(npair)]
    y = [jnp.concatenate([ys[ci][p] for ci in range(nch)], axis=0) for p in range(npair)]
    rk = [r_ref[:, cs].astype(F32) * k_ref[:, cs].astype(F32) * rk_ref[:, cs] for cs in cols]
    mean = _stacked_dot(y, gavg)
    rk_sum = _stacked_dot(rk, gsum)
    yc = _each(lambda a, b: a - b, y, mean)
    var = _stacked_dot([x * x for x in yc], gavg)
    for p, cs in enumerate(cols):
        yn = yc[p] * lax.rsqrt(var[p] + LNX_EPS) * lnw_ref[:, cs] + lnb_ref[:, cs]
        bonus = rk_sum[p] * v_ref[:, cs].astype(F32)
        y_ref[:, cs] = ((yn + bonus) * g_ref[:, cs].astype(F32)).astype(y_ref.dtype)


def _rwkv_mix_kernel(*refs, tt, npair, has_vres):
    n_prep = 16 if has_vres else 13
    prep_in = refs[0:n_prep]
    rk_ref, lnw_ref, lnb_ref = refs[n_prep:n_prep + 3]
    n_out = 1 if has_vres else 2
    outs = refs[n_prep + 3:n_prep + 3 + n_out]
    s_ref, ca_ref, cc_ref, cb_ref = refs[n_prep + 3 + n_out:n_prep + 7 + n_out]
    work = refs[n_prep + 7 + n_out:]
    _prep_compute(prep_in, work, (ca_ref, cc_ref, cb_ref), has_vres=has_vres, tt=tt)
    if not has_vres:
        outs[1][...] = work[2][...].astype(outs[1].dtype)
    _scan_compute(*work, rk_ref, lnw_ref, lnb_ref, outs[0], s_ref, tt=tt, npair=npair)


def _rwkv_mix(proj, vfirst, lp, r_k, lnx_w, lnx_b, *, bsz, seq, tt):
    n = bsz * seq
    nt = seq // tt
    has_vres = vfirst is not None
    d = D_RWKV
    npair = d // LANES
    row = lambda b, t: (b * nt + t, 0)
    const = lambda b, t: (0, 0)
    rowblk = pl.BlockSpec((tt, d), row)
    par = pl.BlockSpec((1, d), const)
    in_specs = [
        pl.BlockSpec((tt, 3 * d), lambda b, t: (b * nt + t, SEG_A // (3 * d))),
        pl.BlockSpec((tt, 256), lambda b, t: (b * nt + t, SEG_C // 256)),
        pl.BlockSpec((tt, 128), lambda b, t: (b * nt + t, SEG_B // 128)),
        pl.BlockSpec((1, 3 * d), const),
        pl.BlockSpec((1, 256), const),
        pl.BlockSpec((1, 128), const),
        par,
        pl.BlockSpec((128, d), const),
        par,
        pl.BlockSpec((128, d), const),
        pl.BlockSpec((256, d), const),
        par,
        par,
    ]
    args = [proj, proj, proj, lp["mu_a"], lp["mu_c"], lp["mu_b"], lp["w0"], lp["wup"], lp["a0"],
            lp["aup"], lp["gup"], lp["k_k"], lp["k_a"]]
    if has_vres:
        in_specs += [rowblk, par, pl.BlockSpec((256, d), const)]
        args += [vfirst, lp["v0"], lp["vup"]]
    in_specs += [par] * 3
    args += [r_k.reshape(1, d), lnx_w.reshape(1, d), lnx_b.reshape(1, d)]
    n_out = 1 if has_vres else 2
    outs = pl.pallas_call(
        functools.partial(_rwkv_mix_kernel, tt=tt, npair=npair, has_vres=has_vres),
        grid=(bsz, nt),
        in_specs=in_specs,
        out_specs=[rowblk] * n_out,
        out_shape=[jax.ShapeDtypeStruct((n, d), BF16)] * n_out,
        scratch_shapes=[pltpu.VMEM((npair, LANES, LANES), F32),
                        pltpu.VMEM((8, 3 * d), F32), pltpu.VMEM((8, 256), F32),
                        pltpu.VMEM((8, 128), F32)] + [pltpu.VMEM((tt, d), F32)] * 7,
        compiler_params=_cparams(("parallel", "arbitrary")),
        name="rwkv_mix",
    )(*args)
    return (outs[0], None) if has_vres else (outs[0], outs[1])


def _sb_kernel(q_ref, k_ref, v_ref, g_ref, o_ref, vab_ref, *, blk, nblk, ppg, scale):
    i = pl.program_id(2)
    lane = lax.broadcasted_iota(jnp.int32, (blk, LANES), 1)
    is_a = lane < HEAD_DIM
    nh = 2 * ppg
    cols = [slice(p * LANES, (p + 1) * LANES) for p in range(ppg)]

    @pl.when(i == 0)
    def _():
        for p in range(ppg):
            for j in range(nblk):
                vb = v_ref[j * blk:(j + 1) * blk, cols[p]]
                vab_ref[p, j, 0:blk, :] = jnp.where(is_a, vb, jnp.zeros_like(vb))
                vab_ref[p, j, blk:2 * blk, :] = jnp.where(is_a, jnp.zeros_like(vb), vb)

    tq = lax.broadcasted_iota(jnp.int32, (blk, blk), 0)
    sk = lax.broadcasted_iota(jnp.int32, (blk, blk), 1)
    causal = sk < tq
    suffix = (tq > sk).astype(BF16)

    q_heads = []
    for p in range(ppg):
        q = q_ref[:, cols[p]].astype(F32) * (scale * LOG2E)
        q_heads += [jnp.where(is_a, q, 0.0).astype(BF16), jnp.where(is_a, 0.0, q).astype(BF16)]

    def block_steps(js, state, diag):
        carries = list(state[0:nh])
        accs = list(state[nh:nh + ppg])
        kbs = [[k_ref[pl.ds(pl.multiple_of(j * blk, blk), blk), cols[p]] for p in range(ppg)]
               for j in js]
        z2 = [[lax.dot_general(q_heads[h], kb[h // 2], NT_DIMS, preferred_element_type=F32)
               for h in range(nh)] for kb in kbs]
        if diag:
            z2[0] = [jnp.where(causal, z, -jnp.inf) for z in z2[0]]
        sp = [[_softplus2(z) for z in zs] for zs in z2]
        spb = [[s.astype(BF16) for s in ss] for ss in sp]
        lsig = [_each(lambda z, s: z - s, zs, ss) for zs, ss in zip(z2, sp)]
        if diag:
            between = [[jnp.dot(s, suffix, preferred_element_type=F32) for s in ss] for ss in spb]
        else:
            bt_all = jnp.dot(jnp.concatenate([s for ss in spb for s in ss], axis=0), suffix,
                             preferred_element_type=F32)
            between = [[bt_all[(b * nh + h) * blk:(b * nh + h + 1) * blk] for h in range(nh)]
                       for b in range(len(js))]
        for b, j in enumerate(js):
            atts = []
            for h in range(nh):
                atts.append(jnp.exp2((lsig[b][h] - between[b][h] - carries[h]).astype(BF16)))
                carries[h] = carries[h] + between[b][h][:, 0:1] + sp[b][h][:, 0:1]
            for p in range(ppg):
                accs[p] = accs[p] + jnp.dot(jnp.concatenate(atts[2 * p:2 * p + 2], axis=1),
                                            vab_ref[p, j], preferred_element_type=F32)
        return tuple(carries) + tuple(accs)

    state = (jnp.zeros((blk, 1), F32),) * nh + (jnp.zeros((blk, LANES), F32),) * ppg
    first = jnp.where(i == 0, 0, 2 - i % 2)
    state = lax.switch(first, [lambda s: block_steps([i], s, True),
                               lambda s: block_steps([i, i - 1], s, True),
                               lambda s: block_steps([i, i - 1, i - 2], s, True)], state)

    def body(m, s):
        j = i - 1 - first - 2 * m
        return block_steps([j, j - 1], s, False)

    state = lax.fori_loop(0, (i - first) // 2, body, state)

    gavg = _head_group_matrix(1.0 / HEAD_DIM)
    accs = [state[nh + p] for p in range(ppg)]
    ms = [_bdot(acc * acc, gavg) for acc in accs]
    for p in range(ppg):
        o_ref[:, cols[p]] = (accs[p] * lax.rsqrt(ms[p] + NORM_EPS)
                             * g_ref[:, cols[p]]).astype(o_ref.dtype)


def _sb_attention(qkv, gain, *, bsz, seq, blk, ppg):
    n = bsz * seq
    nq = seq // blk
    ngrp = D_SB // (LANES * ppg)
    w = LANES * ppg
    return pl.pallas_call(
        functools.partial(_sb_kernel, blk=blk, nblk=nq, ppg=ppg, scale=1.0 / (HEAD_DIM ** 0.5)),
        grid=(bsz, ngrp, nq),
        in_specs=[
            pl.BlockSpec((blk, w), lambda b, h, i: (b * nq + i, h)),
            pl.BlockSpec((seq, w), lambda b, h, i: (b, ngrp + h)),
            pl.BlockSpec((seq, w), lambda b, h, i: (b, 2 * ngrp + h)),
            pl.BlockSpec((1, w), lambda b, h, i: (0, h)),
        ],
        out_specs=pl.BlockSpec((blk, w), lambda b, h, i: (b * nq + i, h)),
        out_shape=jax.ShapeDtypeStruct((n, D_SB), BF16),
        scratch_shapes=[pltpu.VMEM((ppg, nq, 2 * blk, LANES), BF16)],
        compiler_params=_cparams(("parallel", "parallel", "arbitrary")),
        name="sb_attention",
    )(qkv, qkv, qkv, gain.reshape(1, D_SB))


C_DW = 3 * D_RWKV
C_DG = C_DW + W_LORA + A_LORA
C_SB = C_DG + G_LORA


def _projection_weights(w_in, w_in_vres):
    depth, d_model, _ = w_in.shape
    zeros = lambda lead, m: jnp.zeros((lead, d_model, m), F32)
    vres = jnp.concatenate([zeros(1, V_LORA), w_in_vres], axis=0)
    w_rwkv = jnp.concatenate([
        w_in[:, :, 0:C_DW],
        w_in[:, :, C_DG:C_SB], vres, zeros(depth, 256 - G_LORA - V_LORA),
        w_in[:, :, C_DW:C_DG],
        zeros(depth, P_RWKV - SEG_B - 128),
    ], axis=2).astype(BF16)
    return w_rwkv, w_in[:, :, C_SB:].astype(BF16)


def _layer_params(l, mu, mu_vres, w0, w_up, a0, a_up, g_up, v0, v_up, k_k, k_a):
    d = D_RWKV
    c_dw, c_dg, c_sb = C_DW, C_DG, C_SB
    zrow = lambda m: jnp.zeros((m,), F32)
    vres_mu = mu_vres[l - 1] if l > 0 else zrow(V_LORA)
    mul = mu[l]
    lp = {
        "mu_a": mul[0:c_dw].reshape(1, -1),
        "mu_c": jnp.concatenate([mul[c_dg:c_sb], vres_mu,
                                 zrow(256 - G_LORA - V_LORA)]).reshape(1, -1),
        "mu_b": mul[c_dw:c_dg].reshape(1, -1),
        "w0": 0.5 * w0[l].reshape(1, d),
        "a0": 0.5 * a0[l].reshape(1, d),
        "k_k": k_k[l].reshape(1, d),
        "k_a": k_a[l].reshape(1, d),
        "wup": jnp.concatenate([0.5 * w_up[l], jnp.zeros((A_LORA, d), F32)],
                               axis=0).astype(BF16),
        "aup": jnp.concatenate([jnp.zeros((W_LORA, d), F32), 0.5 * a_up[l]],
                               axis=0).astype(BF16),
        "gup": jnp.concatenate([g_up[l], jnp.zeros((256 - G_LORA, d), F32)],
                               axis=0).astype(BF16),
    }
    if l > 0:
        lp["v0"] = 0.5 * v0[l - 1].reshape(1, d)
        lp["vup"] = jnp.concatenate([jnp.zeros((G_LORA, d), F32), 0.5 * v_up[l - 1],
                                     jnp.zeros((256 - G_LORA - V_LORA, d), F32)],
                                    axis=0).astype(BF16)
    return lp


def kernel(x, pre_mix_g, post_mix_g, pre_mlp_g, post_mlp_g, w_in, w_in_vres, mu, mu_vres, w0, w_up,
           a0, a_up, g_up, v0, v_up, k_k, k_a, r_k, lnx_w, lnx_b, sb_out_g, w_out, w_ff_up,
           w_ff_down):
    bsz, seq, d_model = x.shape
    depth = w_in.shape[0]
    n = bsz * seq
    assert w_in.shape[2] == 3 * D_RWKV + W_LORA + A_LORA + G_LORA + 3 * D_SB
    tm_mm = min(1024, n)
    tm_res = min(512, n)
    xf = x.reshape(n, d_model)
    h = _norm_cast(xf, pre_mix_g[0], tm=tm_res)
    w_out_b, w_up_b, w_down_b = (w.astype(BF16) for w in (w_out, w_ff_up, w_ff_down))
    w_rwkv_b, w_sb_b = _projection_weights(w_in, w_in_vres)
    vfirst = None
    for l in range(depth):
        lp = _layer_params(l, mu, mu_vres, w0, w_up, a0, a_up, g_up, v0, v_up, k_k, k_a)
        proj = _matmul(h, w_rwkv_b, l, relu2=False, tm=tm_mm, tn=P_RWKV // 2, name="proj_rwkv")
        qkv = _matmul(h, w_sb_b, l, relu2=False, tm=tm_mm, tn=1536, name="proj_sb")
        y_r, v_l = _rwkv_mix(proj, vfirst, lp, r_k[l], lnx_w[l], lnx_b[l],
                             bsz=bsz, seq=seq, tt=min(256, seq))
        if l == 0:
            vfirst = v_l
        y_s = _sb_attention(qkv, sb_out_g[l], bsz=bsz, seq=seq, blk=min(256, seq), ppg=4)
        xf, h = _mix_out(y_r, y_s, w_out_b, l, xf, post_mix_g[l], pre_mlp_g[l], tm=tm_res)
        ff = _matmul(h, w_up_b, l, relu2=True, tm=tm_mm, tn=2048, name="ffn_up")
        g_next = pre_mix_g[l + 1] if l + 1 < depth else None
        xf, h = _ffn_down(ff, w_down_b, l, xf, post_mlp_g[l], g_next, tm=tm_res, tk=2048)
    return xf.reshape(bsz, seq, d_model)
```
